```python
import math
import jax, jax.numpy as jnp
from jax import lax
import numpy as np

D_MODEL = 1024
BATCH = 1
SEQ = 16384
DEPTH = 1
DEC_BATCH = 16
DEC_SEQ = 16
PAST_LEN = 2048

CHUNK = 64
N_HEADS_A = 8
HEAD_DIM = 64
V_DIM = 2 * HEAD_DIM
D_A = N_HEADS_A * V_DIM
QK_WIDTH = N_HEADS_A * 2 * HEAD_DIM
ROPE_DIM = HEAD_DIM // 4
ROPE_THETA = 500000.0
Q_BLOCK = 128
GMLP_CHUNK = 128
N_GROUPS_B = 4
D_B = 1024
GROUP_DIM = D_B // N_GROUPS_B
D_FF = 2816
IN_COLS = 3 * QK_WIDTH + 2 * D_B + 2 * D_MODEL
EPS = 1e-6
SCALE = HEAD_DIM ** -0.5
NEG_INF = -1e30

kernel_name = "hybrid_diffattn_gmlp_streaming_step"


def rmsnorm(x, g):
    x32 = x.astype(jnp.float32)
    y = x32 * lax.rsqrt(jnp.mean(x32 * x32, axis=-1, keepdims=True) + EPS)
    return (y * g.astype(jnp.float32)).astype(x.dtype)


def swiglu(x, w_in, w_out):
    a, b = jnp.split(x @ w_in, 2, axis=-1)
    return (jax.nn.silu(a) * b) @ w_out


def partial_rope(x, pos):
    inv_freq = ROPE_THETA ** (-jnp.arange(0, ROPE_DIM, 2, dtype=jnp.float32) / ROPE_DIM)
    ang = pos.astype(jnp.float32)[:, None] * inv_freq[None, :]
    cos = jnp.cos(ang)[None, :, None, None, :].astype(x.dtype)
    sin = jnp.sin(ang)[None, :, None, None, :].astype(x.dtype)
    half = ROPE_DIM // 2
    x1, x2, xp = x[..., :half], x[..., half:ROPE_DIM], x[..., ROPE_DIM:]
    return jnp.concatenate([x1 * cos - x2 * sin, x2 * cos + x1 * sin, xp], axis=-1)


def diff_attn_core(q, k, v, lam, mask):
    s = jnp.einsum('bqhid,bkhid->bihqk', q, k).astype(jnp.float32) * SCALE
    if mask is not None:
        s = jnp.where(mask, s, NEG_INF)
    a = jax.nn.softmax(s, axis=-1)
    a = a[:, 0] - lam * a[:, 1]
    return jnp.einsum('bhqk,bkhv->bqhv', a.astype(v.dtype), v)


def prompt_diff_attn(q, k, v, lam):
    B, S = q.shape[0], q.shape[1]
    nb = S // Q_BLOCK
    qb = jnp.moveaxis(q.reshape(B, nb, Q_BLOCK, N_HEADS_A, 2, HEAD_DIM), 1, 0)
    k_chunk = jnp.arange(S) // CHUNK

    def one_block(args):
        i, qi = args
        q_chunk = (i * Q_BLOCK + jnp.arange(Q_BLOCK)) // CHUNK
        mask = k_chunk[None, :] <= q_chunk[:, None]
        return diff_attn_core(qi, k, v, lam, mask)

    o = lax.map(one_block, (jnp.arange(nb), qb))
    return jnp.moveaxis(o, 0, 1).reshape(B, S, N_HEADS_A, V_DIM)


def spatial_mix(vn, w_s, b_s):
    B, S, _ = vn.shape
    L = min(S, GMLP_CHUNK)
    nc = S // L
    w = (w_s * jnp.tril(jnp.ones((GMLP_CHUNK, GMLP_CHUNK), w_s.dtype)))[:, :L, :L]
    vr = vn.reshape(B, nc, L, N_GROUPS_B, GROUP_DIM)
    out = jnp.einsum('gij,bnjgc->bnigc', w, vr) + b_s[:, :L].T[None, None, :, :, None]
    return out.reshape(B, S, D_B)


def layer(x, pos, cache_k, cache_v, p, lam_init):
    B, S, _ = x.shape
    h = x + 0.5 * rmsnorm(swiglu(rmsnorm(x, p['ln_ffn1_pre']), p['w_ffn1_in'], p['w_ffn1_out']), p['ln_ffn1_post'])
    n = rmsnorm(h, p['ln_mix_pre'])
    proj = n @ p['w_in']
    splits = np.cumsum([QK_WIDTH, QK_WIDTH, D_A, D_B, D_B, D_MODEL]).tolist()
    q, k, v, u, vg, ga, gb = jnp.split(proj, splits, axis=-1)
    q = partial_rope(q.reshape(B, S, N_HEADS_A, 2, HEAD_DIM), pos)
    k = partial_rope(k.reshape(B, S, N_HEADS_A, 2, HEAD_DIM), pos)
    v = v.reshape(B, S, N_HEADS_A, V_DIM)
    k_rows = k.reshape(B, S, N_HEADS_A, 2 * HEAD_DIM)
    lam = (jnp.exp(jnp.sum(p['lambda_q1'].astype(jnp.float32) * p['lambda_k1'].astype(jnp.float32)))
           - jnp.exp(jnp.sum(p['lambda_q2'].astype(jnp.float32) * p['lambda_k2'].astype(jnp.float32)))
           + lam_init)
    if cache_k is None:
        attn = prompt_diff_attn(q, k, v, lam)
    else:
        P = cache_k.shape[1]
        k_all = jnp.concatenate([cache_k.reshape(B, P, N_HEADS_A, 2, HEAD_DIM), k], axis=1)
        v_all = jnp.concatenate([cache_v, v], axis=1)
        attn = diff_attn_core(q, k_all, v_all, lam, None)
    attn = (rmsnorm(attn, p['ln_subln']) * (1.0 - lam_init)).reshape(B, S, D_A)
    u = jax.nn.gelu(u)
    vn = rmsnorm(jax.nn.gelu(vg), p['ln_sgu'])
    sgu = u * spatial_mix(vn, p['w_spatial'], p['b_spatial'])
    merged = jax.nn.sigmoid(ga) * (attn @ p['w_proj_a']) + jax.nn.sigmoid(gb) * (sgu @ p['w_proj_b'])
    h = h + rmsnorm(merged @ p['w_out'], p['ln_mix_post'])
    y = h + 0.5 * rmsnorm(swiglu(rmsnorm(h, p['ln_ffn2_pre']), p['w_ffn2_in'], p['w_ffn2_out']), p['ln_ffn2_post'])
    return y, k_rows, v, vn


def setup_inputs(seed: int = 0) -> dict:
    key = jax.random.key(seed)
    ks = jax.random.split(key, 32)

    def nrm(k, shape, scale):
        return jax.random.normal(k, shape, jnp.float32) * scale

    def gain(k, dim):
        return 1.0 + 0.05 * jax.random.normal(k, (DEPTH, dim), jnp.float32)

    return {
        'x_prompt': nrm(ks[0], (BATCH, SEQ, D_MODEL), 1.0),
        'x_sample': nrm(ks[1], (DEC_BATCH, DEC_SEQ, D_MODEL), 1.0),
        'cache_k': nrm(ks[2], (DEPTH, DEC_BATCH, PAST_LEN, N_HEADS_A, 2 * HEAD_DIM), 1.0),
        'cache_v': nrm(ks[3], (DEPTH, DEC_BATCH, PAST_LEN, N_HEADS_A, V_DIM), 1.0),
        'ln_ffn1_pre': gain(ks[4], D_MODEL),
        'w_ffn1_in': nrm(ks[5], (DEPTH, D_MODEL, 2 * D_FF), D_MODEL ** -0.5),
        'w_ffn1_out': nrm(ks[6], (DEPTH, D_FF, D_MODEL), D_FF ** -0.5),
        'ln_ffn1_post': gain(ks[7], D_MODEL),
        'ln_mix_pre': gain(ks[8], D_MODEL),
        'w_in': nrm(ks[9], (DEPTH, D_MODEL, IN_COLS), D_MODEL ** -0.5),
        'lambda_q1': nrm(ks[10], (DEPTH, HEAD_DIM), 0.1),
        'lambda_k1': nrm(ks[11], (DEPTH, HEAD_DIM), 0.1),
        'lambda_q2': nrm(ks[12], (DEPTH, HEAD_DIM), 0.1),
        'lambda_k2': nrm(ks[13], (DEPTH, HEAD_DIM), 0.1),
        'ln_subln': gain(ks[14], V_DIM),
        'ln_sgu': gain(ks[15], D_B),
        'w_spatial': nrm(ks[16], (DEPTH, N_GROUPS_B, GMLP_CHUNK, GMLP_CHUNK), GMLP_CHUNK ** -0.5),
        'b_spatial': 1.0 + 0.1 * jax.random.normal(ks[17], (DEPTH, N_GROUPS_B, GMLP_CHUNK), jnp.float32),
        'w_proj_a': nrm(ks[18], (DEPTH, D_A, D_MODEL), D_A ** -0.5),
        'w_proj_b': nrm(ks[19], (DEPTH, D_B, D_MODEL), D_B ** -0.5),
        'w_out': nrm(ks[20], (DEPTH, D_MODEL, D_MODEL), D_MODEL ** -0.5),
        'ln_mix_post': gain(ks[21], D_MODEL),
        'ln_ffn2_pre': gain(ks[22], D_MODEL),
        'w_ffn2_in': nrm(ks[23], (DEPTH, D_MODEL, 2 * D_FF), D_MODEL ** -0.5),
        'w_ffn2_out': nrm(ks[24], (DEPTH, D_FF, D_MODEL), D_FF ** -0.5),
        'ln_ffn2_post': gain(ks[25], D_MODEL),
    }


def reference(x_prompt, x_sample, cache_k, cache_v,
              ln_ffn1_pre, w_ffn1_in, w_ffn1_out, ln_ffn1_post,
              ln_mix_pre, w_in, lambda_q1, lambda_k1, lambda_q2, lambda_k2,
              ln_subln, ln_sgu, w_spatial, b_spatial, w_proj_a, w_proj_b, w_out, ln_mix_post,
              ln_ffn2_pre, w_ffn2_in, w_ffn2_out, ln_ffn2_post):
    S = x_prompt.shape[1]
    P = cache_k.shape[2]
    L = x_sample.shape[1]
    pos_prompt = jnp.arange(S)
    pos_sample = P + jnp.arange(L)
    hp, hs = x_prompt, x_sample
    kp_l, vp_l, ks_l, vs_l, gs_l = [], [], [], [], []
    for l in range(DEPTH):
        p = {
            'ln_ffn1_pre': ln_ffn1_pre[l], 'w_ffn1_in': w_ffn1_in[l], 'w_ffn1_out': w_ffn1_out[l],
            'ln_ffn1_post': ln_ffn1_post[l], 'ln_mix_pre': ln_mix_pre[l], 'w_in': w_in[l],
            'lambda_q1': lambda_q1[l], 'lambda_k1': lambda_k1[l],
            'lambda_q2': lambda_q2[l], 'lambda_k2': lambda_k2[l],
            'ln_subln': ln_subln[l], 'ln_sgu': ln_sgu[l], 'w_spatial': w_spatial[l], 'b_spatial': b_spatial[l],
            'w_proj_a': w_proj_a[l], 'w_proj_b': w_proj_b[l], 'w_out': w_out[l], 'ln_mix_post': ln_mix_post[l],
            'ln_ffn2_pre': ln_ffn2_pre[l], 'w_ffn2_in': w_ffn2_in[l], 'w_ffn2_out': w_ffn2_out[l],
            'ln_ffn2_post': ln_ffn2_post[l],
        }
        lam_init = 0.8 - 0.6 * math.exp(-0.3 * l)
        hp, kp, vp, _ = layer(hp, pos_prompt, None, None, p, lam_init)
        hs, ksm, vsm, gsm = layer(hs, pos_sample, cache_k[l], cache_v[l], p, lam_init)
        kp_l.append(kp); vp_l.append(vp); ks_l.append(ksm); vs_l.append(vsm); gs_l.append(gsm)
    new_k_prompt = jnp.stack(kp_l)
    new_v_prompt = jnp.stack(vp_l)
    new_k_sample = jnp.stack(ks_l)
    new_v_sample = jnp.stack(vs_l)
    new_sgu_v_sample = jnp.stack(gs_l)
    return (hp, hs, new_k_prompt, new_v_prompt, new_k_sample, new_v_sample, new_sgu_v_sample)
```

```python
import functools
import math

import jax
import jax.numpy as jnp
from jax import lax
from jax.experimental import pallas as pl
from jax.experimental.pallas import tpu as pltpu

F32 = jnp.float32
BF16 = jnp.bfloat16

D_MODEL = 1024
N_HEADS = 8
HEAD_DIM = 64
V_DIM = 2 * HEAD_DIM
ROPE_DIM = HEAD_DIM // 4
ROPE_THETA = 500000.0
CHUNK = 64
GMLP_CHUNK = 128
N_GROUPS = 4
D_B = 1024
GROUP_DIM = D_B // N_GROUPS
D_FF = 2816
EPS = 1e-6
SCALE = HEAD_DIM ** -0.5
LOG2E = 1.4426950408889634
NEG_INF = -1e30
N_SEG = 7

LANES = 128
VMEM_LIMIT_BYTES = 60 * 1024 * 1024


def _rms(x, g):
    return x * lax.rsqrt(jnp.mean(x * x, axis=-1, keepdims=True) + EPS) * g


def _sigmoid(x):
    return 1.0 / (1.0 + jnp.exp(-x))


def _resident(shape):
    zeros = (0,) * len(shape)
    return pl.BlockSpec(shape, lambda *_: zeros, pipeline_mode=pl.Buffered(1))


def _params(n_axes):
    return pltpu.CompilerParams(
        dimension_semantics=("arbitrary",) * n_axes, vmem_limit_bytes=VMEM_LIMIT_BYTES)


def _ffn_kernel(x_ref, gpre_ref, wa_ref, wb_ref, wo_ref, gpost_ref, o_ref):
    x = x_ref[...]
    n = _rms(x, gpre_ref[...]).astype(BF16)
    a = jnp.dot(n, wa_ref[...], preferred_element_type=F32)
    b = jnp.dot(n, wb_ref[...], preferred_element_type=F32)
    g = (a * _sigmoid(a) * b).astype(BF16)
    f = jnp.dot(g, wo_ref[...], preferred_element_type=F32)
    o_ref[...] = x + 0.5 * _rms(f, gpost_ref[...])


def _ffn(x, gpre, wa, wb, wo, gpost, *, tm):
    rows = x.shape[0]
    row_spec = pl.BlockSpec((tm, D_MODEL), lambda i: (i, 0))
    return pl.pallas_call(
        _ffn_kernel,
        grid=(rows // tm,),
        in_specs=[row_spec, _resident(gpre.shape), _resident(wa.shape), _resident(wb.shape),
                  _resident(wo.shape), _resident(gpost.shape)],
        out_specs=row_spec,
        out_shape=jax.ShapeDtypeStruct((rows, D_MODEL), F32),
        compiler_params=_params(1),
        name="ffn",
    )(x, gpre, wa, wb, wo, gpost)


def _gelu(x):
    return jax.nn.gelu(x, approximate=True)


def _mix_in_kernel(h_ref, gpre_ref, w_ref, gsgu_ref, rc_ref, rs1_ref, rs2_ref,
                   q_ref, kf_ref, kb_ref, vf_ref, vb_ref, u_ref, vn_ref, ga_ref, gb_ref):
    n = _rms(h_ref[...], gpre_ref[...]).astype(BF16)

    def seg(j):
        return jnp.dot(n, w_ref[:, j * D_MODEL:(j + 1) * D_MODEL], preferred_element_type=F32)

    rc, rs1, rs2 = rc_ref[...], rs1_ref[...], rs2_ref[...]

    def rope(x):
        return x * rc + pltpu.roll(x, ROPE_DIM // 2, 1) * rs1 + pltpu.roll(x, LANES - ROPE_DIM // 2, 1) * rs2

    q = seg(0)
    for hd in range(N_HEADS):
        sl = slice(hd * V_DIM, (hd + 1) * V_DIM)
        q_ref[:, sl] = (rope(q[:, sl]) * (SCALE * LOG2E)).astype(BF16)
    k = seg(1)
    for hd in range(N_HEADS):
        sl = slice(hd * V_DIM, (hd + 1) * V_DIM)
        kr = rope(k[:, sl])
        kf_ref[:, sl] = kr
        kb_ref[:, sl] = kr.astype(BF16)
    v = seg(2)
    vf_ref[...] = v
    vb_ref[...] = v.astype(BF16)
    u_ref[...] = _gelu(seg(3)).astype(u_ref.dtype)
    vn_ref[...] = _rms(_gelu(seg(4)), gsgu_ref[...]).astype(vn_ref.dtype)
    ga_ref[...] = _sigmoid(seg(5)).astype(ga_ref.dtype)
    gb_ref[...] = _sigmoid(seg(6)).astype(gb_ref.dtype)


def _mix_in(h, gpre, w_in, gsgu, rope_tabs, *, tm, vn_dtype):
    rows = h.shape[0]
    row_spec = pl.BlockSpec((tm, D_MODEL), lambda i: (i, 0))
    tab_spec = pl.BlockSpec((tm, LANES), lambda i: (i, 0))
    out_dtypes = [BF16, F32, BF16, F32, BF16, BF16, vn_dtype, BF16, BF16]
    return pl.pallas_call(
        _mix_in_kernel,
        grid=(rows // tm,),
        in_specs=[row_spec, _resident(gpre.shape), _resident(w_in.shape), _resident(gsgu.shape),
                  tab_spec, tab_spec, tab_spec],
        out_specs=[row_spec] * len(out_dtypes),
        out_shape=[jax.ShapeDtypeStruct((rows, D_MODEL), dt) for dt in out_dtypes],
        compiler_params=_params(1),
        name="mix_in",
    )(h, gpre, w_in, gsgu, *rope_tabs)


def _rope_tables(pos):
    n = pos.shape[0]
    half = ROPE_DIM // 2
    inv_freq = ROPE_THETA ** (-jnp.arange(0, ROPE_DIM, 2, dtype=F32) / ROPE_DIM)
    ang = pos.astype(F32)[:, None] * inv_freq[None, :]
    cos, sin = jnp.cos(ang), jnp.sin(ang)
    rest = HEAD_DIM - ROPE_DIM
    zh, zr, ones = jnp.zeros((n, half), F32), jnp.zeros((n, rest), F32), jnp.ones((n, rest), F32)
    rc = jnp.concatenate([cos, cos, ones], axis=1)
    rs1 = jnp.concatenate([zh, sin, zr], axis=1)
    rs2 = jnp.concatenate([-sin, zh, zr], axis=1)
    return tuple(jnp.concatenate([t, t], axis=1) for t in (rc, rs1, rs2))


def _lambda(lq1_ref, lk1_ref, lq2_ref, lk2_ref, lam_init):
    e1 = jnp.exp(jnp.sum(lq1_ref[...] * lk1_ref[...], axis=1, keepdims=True))
    e2 = jnp.exp(jnp.sum(lq2_ref[...] * lk2_ref[...], axis=1, keepdims=True))
    return e1 - e2 + lam_init


def _stack_subheads(q):
    lane = lax.broadcasted_iota(jnp.int32, q.shape, 1)
    zero = jnp.zeros_like(q)
    return jnp.concatenate([jnp.where(lane < HEAD_DIM, q, zero), jnp.where(lane >= HEAD_DIM, q, zero)], axis=0)


def _scores(qs, kb):
    return lax.dot_general(qs, kb, (((1,), (1,)), ((), ())), preferred_element_type=F32)


def _finish_head(o, t, lam, gsub, lam_init):
    d = o[:t] - lam * o[t:]
    return _rms(d, gsub) * (1.0 - lam_init)


def _attn_kernel(lq1_ref, lk1_ref, lq2_ref, lk2_ref, gsub_ref, q_ref, k_ref, v_ref, o_ref,
                 m_scr, l_scr, acc_scr, *, tq, tk, lam_init):
    i = pl.program_id(1)
    qs = _stack_subheads(q_ref[...])
    m_scr[...] = jnp.full(m_scr.shape, NEG_INF, F32)
    l_scr[...] = jnp.zeros(l_scr.shape, F32)
    acc_scr[...] = jnp.zeros(acc_scr.shape, F32)

    def block(start, mask):
        kb = k_ref[pl.ds(start, tk), :]
        vb = v_ref[pl.ds(start, tk), :]
        s = _scores(qs, kb)
        if mask is not None:
            s = jnp.where(mask, s, NEG_INF)
        m_prev = m_scr[...]
        m_new = jnp.maximum(m_prev, jnp.max(s, axis=1, keepdims=True))
        alpha = jnp.exp2(m_prev - m_new)
        p = jnp.exp2(s - m_new[:, :1])
        l_scr[...] = alpha * l_scr[...] + jnp.sum(p, axis=1, keepdims=True)
        acc_scr[...] = alpha * acc_scr[...] + jnp.dot(p.astype(BF16), vb, preferred_element_type=F32)
        m_scr[...] = m_new

    def body(j, carry):
        block(pl.multiple_of(j * tk, tk), None)
        return carry

    lax.fori_loop(0, (i * tq) // tk, body, 0)

    row = lax.broadcasted_iota(jnp.int32, (2 * tq, tk), 0)
    col = lax.broadcasted_iota(jnp.int32, (2 * tq, tk), 1)
    q_chunk = jnp.where(row >= tq, row - tq, row) // CHUNK
    for d in range(tq // tk):
        mask = (col + d * tk) // CHUNK <= q_chunk
        block(pl.multiple_of(i * tq + d * tk, tk), mask)

    lam = _lambda(lq1_ref, lk1_ref, lq2_ref, lk2_ref, lam_init)
    o = acc_scr[...] / l_scr[...]
    o_ref[...] = _finish_head(o, tq, lam, gsub_ref[...], lam_init).astype(o_ref.dtype)


def _prompt_attention(lams, gsub, q, k, v, *, tq, tk, lam_init):
    s = q.shape[0]
    vec_spec = pl.BlockSpec((1, HEAD_DIM), lambda h, i: (0, 0))
    return pl.pallas_call(
        functools.partial(_attn_kernel, tq=tq, tk=tk, lam_init=lam_init),
        grid=(N_HEADS, s // tq),
        in_specs=[vec_spec] * 4 + [
            pl.BlockSpec((1, V_DIM), lambda h, i: (0, 0)),
            pl.BlockSpec((tq, V_DIM), lambda h, i: (i, h)),
            pl.BlockSpec((s, V_DIM), lambda h, i: (0, h)),
            pl.BlockSpec((s, V_DIM), lambda h, i: (0, h)),
        ],
        out_specs=pl.BlockSpec((tq, V_DIM), lambda h, i: (i, h)),
        out_shape=jax.ShapeDtypeStruct((s, N_HEADS * V_DIM), BF16),
        scratch_shapes=[pltpu.VMEM((2 * tq, LANES), F32)] * 3,
        compiler_params=_params(2),
        name="prompt_attn",
    )(*lams, gsub, q, k, v)


def _sample_attn_kernel(lq1_ref, lk1_ref, lq2_ref, lk2_ref, gsub_ref, q_ref, kn_ref, vn_ref,
                        ck_ref, cv_ref, o_ref, *, heads, t, lam_init):
    lam = _lambda(lq1_ref, lk1_ref, lq2_ref, lk2_ref, lam_init)
    for hd in range(heads):
        sl = slice(hd * V_DIM, (hd + 1) * V_DIM)
        qs = _stack_subheads(q_ref[:, sl])
        s_c = _scores(qs, ck_ref[0, :, sl].astype(BF16))
        s_n = _scores(qs, kn_ref[:, sl])
        m = jnp.maximum(jnp.max(s_c, axis=1, keepdims=True), jnp.max(s_n, axis=1, keepdims=True))
        p_c = jnp.exp2(s_c - m)
        p_n = jnp.exp2(s_n - m)
        l = jnp.sum(p_c, axis=1, keepdims=True) + jnp.sum(p_n, axis=1, keepdims=True)
        acc = jnp.dot(p_c.astype(BF16), cv_ref[0, :, sl].astype(BF16), preferred_element_type=F32)
        acc = acc + jnp.dot(p_n.astype(BF16), vn_ref[:, sl], preferred_element_type=F32)
        o_ref[:, sl] = _finish_head(acc / l, t, lam, gsub_ref[...], lam_init).astype(o_ref.dtype)


def _sample_attention(lams, gsub, q, k_new, v_new, cache_k, cache_v, *, heads_per_step, lam_init):
    nb, past, width = cache_k.shape
    t = q.shape[0] // nb
    w = heads_per_step * V_DIM
    vec_spec = pl.BlockSpec((1, HEAD_DIM), lambda b, g: (0, 0))
    new_spec = pl.BlockSpec((t, w), lambda b, g: (b, g))
    cache_spec = pl.BlockSpec((1, past, w), lambda b, g: (b, 0, g))
    return pl.pallas_call(
        functools.partial(_sample_attn_kernel, heads=heads_per_step, t=t, lam_init=lam_init),
        grid=(nb, width // w),
        in_specs=[vec_spec] * 4 + [pl.BlockSpec((1, V_DIM), lambda b, g: (0, 0)),
                                   new_spec, new_spec, new_spec, cache_spec, cache_spec],
        out_specs=new_spec,
        out_shape=jax.ShapeDtypeStruct(q.shape, BF16),
        compiler_params=_params(2),
        name="sample_attn",
    )(*lams, gsub, q, k_new, v_new, cache_k, cache_v)


def _merge_kernel(h_ref, attn_ref, u_ref, vn_ref, ga_ref, gb_ref, ws_ref, bs_ref,
                  wpa_ref, wpb_ref, wo_ref, gpost_ref, o_ref, sgu_scr, *, tm, chunk):
    r = lax.broadcasted_iota(jnp.int32, (chunk, chunk), 0)
    c = lax.broadcasted_iota(jnp.int32, (chunk, chunk), 1)
    for g in range(N_GROUPS):
        w = jnp.where(c <= r, ws_ref[g], 0.0).astype(BF16)
        bias = bs_ref[g]
        gl = slice(g * GROUP_DIM, (g + 1) * GROUP_DIM)
        for ci in range(tm // chunk):
            rs = slice(ci * chunk, (ci + 1) * chunk)
            mixed = jnp.dot(w, vn_ref[rs, gl].astype(BF16), preferred_element_type=F32) + bias
            sgu_scr[rs, gl] = (u_ref[rs, gl].astype(F32) * mixed).astype(BF16)
    pa = jnp.dot(attn_ref[...], wpa_ref[...], preferred_element_type=F32)
    pb = jnp.dot(sgu_scr[...], wpb_ref[...], preferred_element_type=F32)
    merged = ga_ref[...].astype(F32) * pa + gb_ref[...].astype(F32) * pb
    mix = jnp.dot(merged.astype(BF16), wo_ref[...], preferred_element_type=F32)
    o_ref[...] = h_ref[...] + _rms(mix, gpost_ref[...])


def _merge(h, attn, u, vn, ga, gb, ws, bs, wpa, wpb, wo, gpost, *, tm, chunk):
    rows = h.shape[0]
    row_spec = pl.BlockSpec((tm, D_MODEL), lambda i: (i, 0))
    return pl.pallas_call(
        functools.partial(_merge_kernel, tm=tm, chunk=chunk),
        grid=(rows // tm,),
        in_specs=[row_spec] * 6 + [_resident(a.shape) for a in (ws, bs, wpa, wpb, wo, gpost)],
        out_specs=row_spec,
        out_shape=jax.ShapeDtypeStruct((rows, D_MODEL), F32),
        scratch_shapes=[pltpu.VMEM((tm, D_B), BF16)],
        compiler_params=_params(1),
        name="merge",
    )(h, attn, u, vn, ga, gb, ws, bs, wpa, wpb, wo, gpost)


def _row_tile(rows, target):
    tm = min(rows, target)
    assert rows % tm == 0, (rows, tm)
    return tm


def kernel(x_prompt, x_sample, cache_k, cache_v, ln_ffn1_pre, w_ffn1_in, w_ffn1_out, ln_ffn1_post, ln_mix_pre, w_in, lambda_q1, lambda_k1, lambda_q2, lambda_k2, ln_subln, ln_sgu, w_spatial, b_spatial, w_proj_a, w_proj_b, w_out, ln_mix_post, ln_ffn2_pre, w_ffn2_in, w_ffn2_out, ln_ffn2_post):
    depth = w_in.shape[0]
    assert depth == 1, "single-layer trunk"
    batch, seq, _ = x_prompt.shape
    dec_batch, dec_seq, _ = x_sample.shape
    past = cache_k.shape[2]
    assert batch == 1 and seq % GMLP_CHUNK == 0 and dec_seq <= GMLP_CHUNK
    lam_init = 0.8 - 0.6 * math.exp(-0.3 * 0)

    row = lambda g: g[0][None, :]
    ffn1 = (row(ln_ffn1_pre), w_ffn1_in[0, :, :D_FF].astype(BF16), w_ffn1_in[0, :, D_FF:].astype(BF16),
            w_ffn1_out[0].astype(BF16), row(ln_ffn1_post))
    ffn2 = (row(ln_ffn2_pre), w_ffn2_in[0, :, :D_FF].astype(BF16), w_ffn2_in[0, :, D_FF:].astype(BF16),
            w_ffn2_out[0].astype(BF16), row(ln_ffn2_post))
    w_in_b = w_in[0].astype(BF16)
    wpa, wpb, wo = w_proj_a[0].astype(BF16), w_proj_b[0].astype(BF16), w_out[0].astype(BF16)
    lams = (lambda_q1, lambda_k1, lambda_q2, lambda_k2)
    gsub = row(ln_subln)

    def layer(x, pos, chunk, attend, tm, vn_dtype):
        h = _ffn(x, *ffn1, tm=tm)
        q, kf, kb, vf, vb, u, vn, ga, gb = _mix_in(
            h, row(ln_mix_pre), w_in_b, row(ln_sgu), _rope_tables(pos), tm=tm, vn_dtype=vn_dtype)
        attn = attend(q, kb, vb)
        ws = w_spatial[0, :, :chunk, :chunk]
        bs = b_spatial[0, :, :chunk, None]
        h = _merge(h, attn, u, vn, ga, gb, ws, bs, wpa, wpb, wo, row(ln_mix_post), tm=tm, chunk=chunk)
        y = _ffn(h, *ffn2, tm=tm)
        return y, kf, vf, vn

    tq = _row_tile(seq, 512)
    yp, kp, vp, _ = layer(
        x_prompt.reshape(seq, D_MODEL), jnp.arange(seq), GMLP_CHUNK,
        lambda q, k, v: _prompt_attention(lams, gsub, q, k, v, tq=tq, tk=tq, lam_init=lam_init),
        _row_tile(seq, 512), BF16)

    rows_s = dec_batch * dec_seq
    ck = cache_k[0].reshape(dec_batch, past, N_HEADS * V_DIM)
    cv = cache_v[0].reshape(dec_batch, past, N_HEADS * V_DIM)
    ys, ks, vs, gs = layer(
        x_sample.reshape(rows_s, D_MODEL), jnp.tile(past + jnp.arange(dec_seq), dec_batch), dec_seq,
        lambda q, k, v: _sample_attention(lams, gsub, q, k, v, ck, cv, heads_per_step=4, lam_init=lam_init),
        _row_tile(rows_s, 256), F32)

    return (yp.reshape(batch, seq, D_MODEL),
            ys.reshape(dec_batch, dec_seq, D_MODEL),
            kp.reshape(1, batch, seq, N_HEADS, V_DIM),
            vp.reshape(1, batch, seq, N_HEADS, V_DIM),
            ks.reshape(1, dec_batch, dec_seq, N_HEADS, V_DIM),
            vs.reshape(1, dec_batch, dec_seq, N_HEADS, V_DIM),
            gs.reshape(1, dec_batch, dec_seq, D_B))
```

```python
import functools
import math

import jax
import jax.numpy as jnp
from jax import lax
from jax.experimental import pallas as pl
from jax.experimental.pallas import tpu as pltpu

F32 = jnp.float32
BF16 = jnp.bfloat16

D_MODEL = 1024
N_HEADS = 8
HEAD_DIM = 64
V_DIM = 2 * HEAD_DIM
ROPE_DIM = HEAD_DIM // 4
ROPE_THETA = 500000.0
CHUNK = 64
GMLP_CHUNK = 128
N_GROUPS = 4
D_B = 1024
GROUP_DIM = D_B // N_GROUPS
D_FF = 2816
EPS = 1e-6
SCALE = HEAD_DIM ** -0.5
LOG2E = 1.4426950408889634
NEG_INF = -1e30
N_SEG = 7

LANES = 128
VMEM_LIMIT_BYTES = 60 * 1024 * 1024


def _rms(x, g):
    return x * lax.rsqrt(jnp.mean(x * x, axis=-1, keepdims=True) + EPS) * g


def _sigmoid(x):
    return 1.0 / (1.0 + jnp.exp(-x))


def _resident(shape):
    zeros = (0,) * len(shape)
    return pl.BlockSpec(shape, lambda *_: zeros, pipeline_mode=pl.Buffered(1))


def _params(n_axes):
    return pltpu.CompilerParams(
        dimension_semantics=("arbitrary",) * n_axes, vmem_limit_bytes=VMEM_LIMIT_BYTES)


def _ffn_kernel(x_ref, gpre_ref, wa_ref, wb_ref, wo_ref, gpost_ref, o_ref):
    x = x_ref[...]
    n = _rms(x, gpre_ref[...]).astype(BF16)
    a = jnp.dot(n, wa_ref[...], preferred_element_type=F32)
    b = jnp.dot(n, wb_ref[...], preferred_element_type=F32)
    g = (a * _sigmoid(a) * b).astype(BF16)
    f = jnp.dot(g, wo_ref[...], preferred_element_type=F32)
    o_ref[...] = x + 0.5 * _rms(f, gpost_ref[...])


def _ffn(x, gpre, wa, wb, wo, gpost, *, tm):
    rows = x.shape[0]
    row_spec = pl.BlockSpec((tm, D_MODEL), lambda i: (i, 0))
    return pl.pallas_call(
        _ffn_kernel,
        grid=(rows // tm,),
        in_specs=[row_spec, _resident(gpre.shape), _resident(wa.shape), _resident(wb.shape),
                  _resident(wo.shape), _resident(gpost.shape)],
        out_specs=row_spec,
        out_shape=jax.ShapeDtypeStruct((rows, D_MODEL), F32),
        compiler_params=_params(1),
        name="ffn",
    )(x, gpre, wa, wb, wo, gpost)


def _gelu(x):
    return jax.nn.gelu(x, approximate=True)


def _mix_in_kernel(h_ref, gpre_ref, w_ref, gsgu_ref, rc_ref, rs1_ref, rs2_ref,
                   q_ref, kf_ref, kb_ref, vf_ref, vb_ref, u_ref, vn_ref, ga_ref, gb_ref, *, transposed_qv):
    n = _rms(h_ref[...], gpre_ref[...]).astype(BF16)

    def put_qv(ref, hd, x):
        if transposed_qv:
            ref[0, hd * V_DIM:(hd + 1) * V_DIM, :] = x.T.astype(BF16)
        else:
            ref[:, hd * V_DIM:(hd + 1) * V_DIM] = x.astype(BF16)

    def seg(j):
        return jnp.dot(n, w_ref[:, j * D_MODEL:(j + 1) * D_MODEL], preferred_element_type=F32)

    rc, rs1, rs2 = rc_ref[...], rs1_ref[...], rs2_ref[...]

    def rope(x):
        return x * rc + pltpu.roll(x, ROPE_DIM // 2, 1) * rs1 + pltpu.roll(x, LANES - ROPE_DIM // 2, 1) * rs2

    q = seg(0)
    for hd in range(N_HEADS):
        sl = slice(hd * V_DIM, (hd + 1) * V_DIM)
        put_qv(q_ref, hd, rope(q[:, sl]) * (SCALE * LOG2E))
    k = seg(1)
    for hd in range(N_HEADS):
        sl = slice(hd * V_DIM, (hd + 1) * V_DIM)
        kr = rope(k[:, sl])
        kf_ref[:, sl] = kr
        kb_ref[:, sl] = kr.astype(BF16)
    v = seg(2)
    vf_ref[...] = v
    for hd in range(N_HEADS):
        put_qv(vb_ref, hd, v[:, hd * V_DIM:(hd + 1) * V_DIM])
    u_ref[...] = _gelu(seg(3)).astype(u_ref.dtype)
    vn_ref[...] = _rms(_gelu(seg(4)), gsgu_ref[...]).astype(vn_ref.dtype)
    ga_ref[...] = _sigmoid(seg(5)).astype(ga_ref.dtype)
    gb_ref[...] = _sigmoid(seg(6)).astype(gb_ref.dtype)


def _mix_in(h, gpre, w_in, gsgu, rope_tabs, *, tm, vn_dtype, transposed_qv):
    rows = h.shape[0]
    row_spec = pl.BlockSpec((tm, D_MODEL), lambda i: (i, 0))
    tab_spec = pl.BlockSpec((tm, LANES), lambda i: (i, 0))
    row_out = lambda dt: (row_spec, jax.ShapeDtypeStruct((rows, D_MODEL), dt))
    if transposed_qv:
        qv_out = (pl.BlockSpec((1, D_MODEL, tm), lambda i: (i, 0, 0)),
                  jax.ShapeDtypeStruct((rows // tm, D_MODEL, tm), BF16))
    else:
        qv_out = row_out(BF16)
    outs = [qv_out, row_out(F32), row_out(BF16), row_out(F32), qv_out, row_out(BF16), row_out(vn_dtype),
            row_out(BF16), row_out(BF16)]
    return pl.pallas_call(
        functools.partial(_mix_in_kernel, transposed_qv=transposed_qv),
        grid=(rows // tm,),
        in_specs=[row_spec, _resident(gpre.shape), _resident(w_in.shape), _resident(gsgu.shape),
                  tab_spec, tab_spec, tab_spec],
        out_specs=[o[0] for o in outs],
        out_shape=[o[1] for o in outs],
        compiler_params=_params(1),
        name="mix_in",
    )(h, gpre, w_in, gsgu, *rope_tabs)


def _rope_tables(pos):
    n = pos.shape[0]
    half = ROPE_DIM // 2
    inv_freq = ROPE_THETA ** (-jnp.arange(0, ROPE_DIM, 2, dtype=F32) / ROPE_DIM)
    ang = pos.astype(F32)[:, None] * inv_freq[None, :]
    cos, sin = jnp.cos(ang), jnp.sin(ang)
    rest = HEAD_DIM - ROPE_DIM
    zh, zr, ones = jnp.zeros((n, half), F32), jnp.zeros((n, rest), F32), jnp.ones((n, rest), F32)
    rc = jnp.concatenate([cos, cos, ones], axis=1)
    rs1 = jnp.concatenate([zh, sin, zr], axis=1)
    rs2 = jnp.concatenate([-sin, zh, zr], axis=1)
    return tuple(jnp.concatenate([t, t], axis=1) for t in (rc, rs1, rs2))


def _lambda(lq1_ref, lk1_ref, lq2_ref, lk2_ref, lam_init):
    e1 = jnp.exp(jnp.sum(lq1_ref[...] * lk1_ref[...], axis=1, keepdims=True))
    e2 = jnp.exp(jnp.sum(lq2_ref[...] * lk2_ref[...], axis=1, keepdims=True))
    return e1 - e2 + lam_init


def _stack_subheads(q):
    lane = lax.broadcasted_iota(jnp.int32, q.shape, 1)
    zero = jnp.zeros_like(q)
    return jnp.concatenate([jnp.where(lane < HEAD_DIM, q, zero), jnp.where(lane >= HEAD_DIM, q, zero)], axis=0)


def _scores(qs, kb):
    return lax.dot_general(qs, kb, (((1,), (1,)), ((), ())), preferred_element_type=F32)


def _finish_head(o, t, lam, gsub, lam_init):
    d = o[:t] - lam * o[t:]
    return _rms(d, gsub) * (1.0 - lam_init)


def _attn_kernel(lq1_ref, lk1_ref, lq2_ref, lk2_ref, gsub_ref, q_ref, k_ref, v_ref, o_ref,
                 sa_scr, sb_scr, mxa_scr, mxb_scr, m_scr, l_scr, acc_scr, *, t, lam_init):
    i = pl.program_id(1)
    sub = 8
    g = t // sub
    qt = q_ref[0]
    feat = lax.broadcasted_iota(jnp.int32, qt.shape, 0)
    zero = jnp.zeros_like(qt)
    qs = jnp.concatenate([jnp.where(feat < HEAD_DIM, qt, zero), jnp.where(feat >= HEAD_DIM, qt, zero)], axis=1)
    m_scr[...] = jnp.full(m_scr.shape, NEG_INF, F32)
    l_scr[...] = jnp.zeros(l_scr.shape, F32)
    acc_scr[...] = jnp.zeros(acc_scr.shape, F32)

    def qk_stage(j, s_ref, mx_ref):
        s = jnp.dot(k_ref[pl.ds(pl.multiple_of(j * t, t), t), :], qs, preferred_element_type=F32)
        s_ref[...] = s
        mx_ref[...] = jnp.max(s.reshape(g, sub, 2 * t), axis=0)

    def softmax_stage(j, s_ref, mx_ref, masked):
        s3 = s_ref[...].reshape(g, sub, 2 * t)
        if masked:
            shape = (g, sub, 2 * t)
            key = lax.broadcasted_iota(jnp.int32, shape, 0) * sub + lax.broadcasted_iota(jnp.int32, shape, 1)
            col = lax.broadcasted_iota(jnp.int32, shape, 2)
            s3 = jnp.where(key // CHUNK <= jnp.where(col >= t, col - t, col) // CHUNK, s3, NEG_INF)
            m8 = jnp.max(s3, axis=0)
        else:
            m8 = mx_ref[...]
        m_prev = m_scr[...]
        m_new = jnp.maximum(m_prev, jnp.max(m8, axis=0, keepdims=True))
        alpha = jnp.exp2(m_prev - m_new)
        p3 = jnp.exp2(s3 - jnp.broadcast_to(m_new, (sub, 2 * t))[None])
        l_scr[...] = alpha * l_scr[...] + jnp.sum(p3, axis=0)
        pv = jnp.dot(v_ref[j], p3.reshape(t, 2 * t).astype(BF16), preferred_element_type=F32)
        acc_scr[...] = alpha * acc_scr[...] + pv
        m_scr[...] = m_new

    qk_stage(0, sa_scr, mxa_scr)

    def body(jj, carry):
        j = 2 * jj
        qk_stage(j + 1, sb_scr, mxb_scr)
        softmax_stage(j, sa_scr, mxa_scr, False)
        qk_stage(j + 2, sa_scr, mxa_scr)
        softmax_stage(j + 1, sb_scr, mxb_scr, False)
        return carry

    lax.fori_loop(0, i // 2, body, 0)

    @pl.when(i % 2 == 1)
    def _():
        qk_stage(i, sb_scr, mxb_scr)
        softmax_stage(i - 1, sa_scr, mxa_scr, False)
        softmax_stage(i, sb_scr, mxb_scr, True)

    @pl.when(i % 2 == 0)
    def _():
        softmax_stage(i, sa_scr, mxa_scr, True)

    lam = _lambda(lq1_ref, lk1_ref, lq2_ref, lk2_ref, lam_init)
    o = acc_scr[...] / jnp.sum(l_scr[...], axis=0, keepdims=True)
    d = (o[:, :t] - lam * o[:, t:]).T
    o_ref[...] = (_rms(d, gsub_ref[...]) * (1.0 - lam_init)).astype(o_ref.dtype)


def _prompt_attention(lams, gsub, qt, k, vt, *, lam_init):
    nt, _, t = qt.shape
    s = nt * t
    vec_spec = pl.BlockSpec((1, HEAD_DIM), lambda h, i: (0, 0))
    return pl.pallas_call(
        functools.partial(_attn_kernel, t=t, lam_init=lam_init),
        grid=(N_HEADS, nt),
        in_specs=[vec_spec] * 4 + [
            pl.BlockSpec((1, V_DIM), lambda h, i: (0, 0)),
            pl.BlockSpec((1, V_DIM, t), lambda h, i: (i, h, 0)),
            pl.BlockSpec((s, V_DIM), lambda h, i: (0, h)),
            pl.BlockSpec((nt, V_DIM, t), lambda h, i: (0, h, 0)),
        ],
        out_specs=pl.BlockSpec((t, V_DIM), lambda h, i: (i, h)),
        out_shape=jax.ShapeDtypeStruct((s, N_HEADS * V_DIM), BF16),
        scratch_shapes=[pltpu.VMEM((t, 2 * t), F32), pltpu.VMEM((t, 2 * t), F32),
                        pltpu.VMEM((8, 2 * t), F32), pltpu.VMEM((8, 2 * t), F32),
                        pltpu.VMEM((1, 2 * t), F32), pltpu.VMEM((8, 2 * t), F32),
                        pltpu.VMEM((V_DIM, 2 * t), F32)],
        compiler_params=_params(2),
        name="prompt_attn",
    )(*lams, gsub, qt, k, vt)


def _sample_attn_kernel(lq1_ref, lk1_ref, lq2_ref, lk2_ref, gsub_ref, q_ref, kn_ref, vn_ref,
                        ck_ref, cv_ref, o_ref, *, heads, t, lam_init):
    lam = _lambda(lq1_ref, lk1_ref, lq2_ref, lk2_ref, lam_init)
    for hd in range(heads):
        sl = slice(hd * V_DIM, (hd + 1) * V_DIM)
        qs = _stack_subheads(q_ref[:, sl])
        s_c = _scores(qs, ck_ref[0, :, sl].astype(BF16))
        s_n = _scores(qs, kn_ref[:, sl])
        m = jnp.maximum(jnp.max(s_c, axis=1, keepdims=True), jnp.max(s_n, axis=1, keepdims=True))
        p_c = jnp.exp2(s_c - m)
        p_n = jnp.exp2(s_n - m)
        l = jnp.sum(p_c, axis=1, keepdims=True) + jnp.sum(p_n, axis=1, keepdims=True)
        acc = jnp.dot(p_c.astype(BF16), cv_ref[0, :, sl].astype(BF16), preferred_element_type=F32)
        acc = acc + jnp.dot(p_n.astype(BF16), vn_ref[:, sl], preferred_element_type=F32)
        o_ref[:, sl] = _finish_head(acc / l, t, lam, gsub_ref[...], lam_init).astype(o_ref.dtype)


def _sample_attention(lams, gsub, q, k_new, v_new, cache_k, cache_v, *, heads_per_step, lam_init):
    nb, past, width = cache_k.shape
    t = q.shape[0] // nb
    w = heads_per_step * V_DIM
    vec_spec = pl.BlockSpec((1, HEAD_DIM), lambda b, g: (0, 0))
    new_spec = pl.BlockSpec((t, w), lambda b, g: (b, g))
    cache_spec = pl.BlockSpec((1, past, w), lambda b, g: (b, 0, g))
    return pl.pallas_call(
        functools.partial(_sample_attn_kernel, heads=heads_per_step, t=t, lam_init=lam_init),
        grid=(nb, width // w),
        in_specs=[vec_spec] * 4 + [pl.BlockSpec((1, V_DIM), lambda b, g: (0, 0)),
                                   new_spec, new_spec, new_spec, cache_spec, cache_spec],
        out_specs=new_spec,
        out_shape=jax.ShapeDtypeStruct(q.shape, BF16),
        compiler_params=_params(2),
        name="sample_attn",
    )(*lams, gsub, q, k_new, v_new, cache_k, cache_v)


def _merge_kernel(h_ref, attn_ref, u_ref, vn_ref, ga_ref, gb_ref, ws_ref, bs_ref,
                  wpa_ref, wpb_ref, wo_ref, gpost_ref, o_ref, sgu_scr, *, tm, chunk):
    r = lax.broadcasted_iota(jnp.int32, (chunk, chunk), 0)
    c = lax.broadcasted_iota(jnp.int32, (chunk, chunk), 1)
    for g in range(N_GROUPS):
        w = jnp.where(c <= r, ws_ref[g], 0.0).astype(BF16)
        bias = bs_ref[g]
        gl = slice(g * GROUP_DIM, (g + 1) * GROUP_DIM)
        for ci in range(tm // chunk):
            rs = slice(ci * chunk, (ci + 1) * chunk)
            mixed = jnp.dot(w, vn_ref[rs, gl].astype(BF16), preferred_element_type=F32) + bias
            sgu_scr[rs, gl] = (u_ref[rs, gl].astype(F32) * mixed).astype(BF16)
    pa = jnp.dot(attn_ref[...], wpa_ref[...], preferred_element_type=F32)
    pb = jnp.dot(sgu_scr[...], wpb_ref[...], preferred_element_type=F32)
    merged = ga_ref[...].astype(F32) * pa + gb_ref[...].astype(F32) * pb
    mix = jnp.dot(merged.astype(BF16), wo_ref[...], preferred_element_type=F32)
    o_ref[...] = h_ref[...] + _rms(mix, gpost_ref[...])


def _merge(h, attn, u, vn, ga, gb, ws, bs, wpa, wpb, wo, gpost, *, tm, chunk):
    rows = h.shape[0]
    row_spec = pl.BlockSpec((tm, D_MODEL), lambda i: (i, 0))
    return pl.pallas_call(
        functools.partial(_merge_kernel, tm=tm, chunk=chunk),
        grid=(rows // tm,),
        in_specs=[row_spec] * 6 + [_resident(a.shape) for a in (ws, bs, wpa, wpb, wo, gpost)],
        out_specs=row_spec,
        out_shape=jax.ShapeDtypeStruct((rows, D_MODEL), F32),
        scratch_shapes=[pltpu.VMEM((tm, D_B), BF16)],
        compiler_params=_params(1),
        name="merge",
    )(h, attn, u, vn, ga, gb, ws, bs, wpa, wpb, wo, gpost)


def _row_tile(rows, target):
    tm = min(rows, target)
    assert rows % tm == 0, (rows, tm)
    return tm


def kernel(x_prompt, x_sample, cache_k, cache_v, ln_ffn1_pre, w_ffn1_in, w_ffn1_out, ln_ffn1_post, ln_mix_pre, w_in, lambda_q1, lambda_k1, lambda_q2, lambda_k2, ln_subln, ln_sgu, w_spatial, b_spatial, w_proj_a, w_proj_b, w_out, ln_mix_post, ln_ffn2_pre, w_ffn2_in, w_ffn2_out, ln_ffn2_post):
    depth = w_in.shape[0]
    assert depth == 1, "single-layer trunk"
    batch, seq, _ = x_prompt.shape
    dec_batch, dec_seq, _ = x_sample.shape
    past = cache_k.shape[2]
    assert batch == 1 and seq % GMLP_CHUNK == 0 and dec_seq <= GMLP_CHUNK
    lam_init = 0.8 - 0.6 * math.exp(-0.3 * 0)

    row = lambda g: g[0][None, :]
    ffn1 = (row(ln_ffn1_pre), w_ffn1_in[0, :, :D_FF].astype(BF16), w_ffn1_in[0, :, D_FF:].astype(BF16),
            w_ffn1_out[0].astype(BF16), row(ln_ffn1_post))
    ffn2 = (row(ln_ffn2_pre), w_ffn2_in[0, :, :D_FF].astype(BF16), w_ffn2_in[0, :, D_FF:].astype(BF16),
            w_ffn2_out[0].astype(BF16), row(ln_ffn2_post))
    w_in_b = w_in[0].astype(BF16)
    wpa, wpb, wo = w_proj_a[0].astype(BF16), w_proj_b[0].astype(BF16), w_out[0].astype(BF16)
    lams = (lambda_q1, lambda_k1, lambda_q2, lambda_k2)
    gsub = row(ln_subln)

    def layer(x, pos, chunk, attend, tm, vn_dtype, transposed_qv):
        h = _ffn(x, *ffn1, tm=tm)
        q, kf, kb, vf, vb, u, vn, ga, gb = _mix_in(
            h, row(ln_mix_pre), w_in_b, row(ln_sgu), _rope_tables(pos), tm=tm, vn_dtype=vn_dtype,
            transposed_qv=transposed_qv)
        attn = attend(q, kb, vb)
        ws = w_spatial[0, :, :chunk, :chunk]
        bs = b_spatial[0, :, :chunk, None]
        h = _merge(h, attn, u, vn, ga, gb, ws, bs, wpa, wpb, wo, row(ln_mix_post), tm=tm, chunk=chunk)
        y = _ffn(h, *ffn2, tm=tm)
        return y, kf, vf, vn

    yp, kp, vp, _ = layer(
        x_prompt.reshape(seq, D_MODEL), jnp.arange(seq), GMLP_CHUNK,
        lambda qt, k, vt: _prompt_attention(lams, gsub, qt, k, vt, lam_init=lam_init),
        _row_tile(seq, 512), BF16, True)

    rows_s = dec_batch * dec_seq
    ck = cache_k[0].reshape(dec_batch, past, N_HEADS * V_DIM)
    cv = cache_v[0].reshape(dec_batch, past, N_HEADS * V_DIM)
    ys, ks, vs, gs = layer(
        x_sample.reshape(rows_s, D_MODEL), jnp.tile(past + jnp.arange(dec_seq), dec_batch), dec_seq,
        lambda q, k, v: _sample_attention(lams, gsub, q, k, v, ck, cv, heads_per_step=4, lam_init=lam_init),
        _row_tile(rows_s, 256), F32, False)

    return (yp.reshape(batch, seq, D_MODEL),
            ys.reshape(dec_batch, dec_seq, D_MODEL),
            kp.reshape(1, batch, seq, N_HEADS, V_DIM),
            vp.reshape(1, batch, seq, N_HEADS, V_DIM),
            ks.reshape(1, dec_batch, dec_seq, N_HEADS, V_DIM),
            vs.reshape(1, dec_batch, dec_seq, N_HEADS, V_DIM),
            gs.reshape(1, dec_batch, dec_seq, D_B))
```

```python
import functools
import math

import jax
import jax.numpy as jnp
from jax import lax
from jax.experimental import pallas as pl
from jax.experimental.pallas import tpu as pltpu

F32 = jnp.float32
BF16 = jnp.bfloat16

D_MODEL = 1024
N_HEADS = 8
HEAD_DIM = 64
V_DIM = 2 * HEAD_DIM
ROPE_DIM = HEAD_DIM // 4
ROPE_THETA = 500000.0
CHUNK = 64
GMLP_CHUNK = 128
N_GROUPS = 4
D_B = 1024
GROUP_DIM = D_B // N_GROUPS
D_FF = 2816
EPS = 1e-6
SCALE = HEAD_DIM ** -0.5
LOG2E = 1.4426950408889634
NEG_INF = -1e30
N_SEG = 7

LANES = 128
VMEM_LIMIT_BYTES = 60 * 1024 * 1024


def _rms(x, g):
    return x * lax.rsqrt(jnp.mean(x * x, axis=-1, keepdims=True) + EPS) * g


def _sigmoid(x):
    return 1.0 / (1.0 + jnp.exp(-x))


def _resident(shape):
    zeros = (0,) * len(shape)
    return pl.BlockSpec(shape, lambda *_: zeros, pipeline_mode=pl.Buffered(1))


def _params(n_axes):
    return pltpu.CompilerParams(
        dimension_semantics=("arbitrary",) * n_axes, vmem_limit_bytes=VMEM_LIMIT_BYTES)


def _ffn_kernel(x_ref, gpre_ref, wa_ref, wb_ref, wo_ref, gpost_ref, o_ref):
    x = x_ref[...]
    n = _rms(x, gpre_ref[...]).astype(BF16)
    a = jnp.dot(n, wa_ref[...], preferred_element_type=F32)
    b = jnp.dot(n, wb_ref[...], preferred_element_type=F32)
    g = (a * _sigmoid(a) * b).astype(BF16)
    f = jnp.dot(g, wo_ref[...], preferred_element_type=F32)
    o_ref[...] = x + 0.5 * _rms(f, gpost_ref[...])


def _ffn(x, gpre, wa, wb, wo, gpost, *, tm):
    rows = x.shape[0]
    row_spec = pl.BlockSpec((tm, D_MODEL), lambda i: (i, 0))
    return pl.pallas_call(
        _ffn_kernel,
        grid=(rows // tm,),
        in_specs=[row_spec, _resident(gpre.shape), _resident(wa.shape), _resident(wb.shape),
                  _resident(wo.shape), _resident(gpost.shape)],
        out_specs=row_spec,
        out_shape=jax.ShapeDtypeStruct((rows, D_MODEL), F32),
        compiler_params=_params(1),
        name="ffn",
    )(x, gpre, wa, wb, wo, gpost)


def _gelu(x):
    return jax.nn.gelu(x, approximate=True)


def _mix_in_kernel(h_ref, gpre_ref, w_ref, gsgu_ref, rc_ref, rs1_ref, rs2_ref,
                   q_ref, kf_ref, kb_ref, vf_ref, vb_ref, u_ref, vn_ref, ga_ref, gb_ref, *, transposed_qv):
    n = _rms(h_ref[...], gpre_ref[...]).astype(BF16)

    def put_qv(ref, hd, x):
        if transposed_qv:
            ref[0, hd * V_DIM:(hd + 1) * V_DIM, :] = x.T.astype(BF16)
        else:
            ref[:, hd * V_DIM:(hd + 1) * V_DIM] = x.astype(BF16)

    def seg(j):
        return jnp.dot(n, w_ref[:, j * D_MODEL:(j + 1) * D_MODEL], preferred_element_type=F32)

    rc, rs1, rs2 = rc_ref[...], rs1_ref[...], rs2_ref[...]

    def rope(x):
        return x * rc + pltpu.roll(x, ROPE_DIM // 2, 1) * rs1 + pltpu.roll(x, LANES - ROPE_DIM // 2, 1) * rs2

    q = seg(0)
    for hd in range(N_HEADS):
        sl = slice(hd * V_DIM, (hd + 1) * V_DIM)
        put_qv(q_ref, hd, rope(q[:, sl]) * (SCALE * LOG2E))
    k = seg(1)
    for hd in range(N_HEADS):
        sl = slice(hd * V_DIM, (hd + 1) * V_DIM)
        kr = rope(k[:, sl])
        kf_ref[:, sl] = kr
        kb_ref[:, sl] = kr.astype(BF16)
    v = seg(2)
    vf_ref[...] = v
    for hd in range(N_HEADS):
        put_qv(vb_ref, hd, v[:, hd * V_DIM:(hd + 1) * V_DIM])
    u_ref[...] = _gelu(seg(3)).astype(u_ref.dtype)
    vn_ref[...] = _rms(_gelu(seg(4)), gsgu_ref[...]).astype(vn_ref.dtype)
    ga_ref[...] = _sigmoid(seg(5)).astype(ga_ref.dtype)
    gb_ref[...] = _sigmoid(seg(6)).astype(gb_ref.dtype)


def _mix_in(h, gpre, w_in, gsgu, rope_tabs, *, tm, vn_dtype, transposed_qv):
    rows = h.shape[0]
    row_spec = pl.BlockSpec((tm, D_MODEL), lambda i: (i, 0))
    tab_spec = pl.BlockSpec((tm, LANES), lambda i: (i, 0))
    row_out = lambda dt: (row_spec, jax.ShapeDtypeStruct((rows, D_MODEL), dt))
    if transposed_qv:
        qv_out = (pl.BlockSpec((1, D_MODEL, tm), lambda i: (i, 0, 0)),
                  jax.ShapeDtypeStruct((rows // tm, D_MODEL, tm), BF16))
    else:
        qv_out = row_out(BF16)
    outs = [qv_out, row_out(F32), row_out(BF16), row_out(F32), qv_out, row_out(BF16), row_out(vn_dtype),
            row_out(BF16), row_out(BF16)]
    return pl.pallas_call(
        functools.partial(_mix_in_kernel, transposed_qv=transposed_qv),
        grid=(rows // tm,),
        in_specs=[row_spec, _resident(gpre.shape), _resident(w_in.shape), _resident(gsgu.shape),
                  tab_spec, tab_spec, tab_spec],
        out_specs=[o[0] for o in outs],
        out_shape=[o[1] for o in outs],
        compiler_params=_params(1),
        name="mix_in",
    )(h, gpre, w_in, gsgu, *rope_tabs)


def _rope_tables(pos):
    half = ROPE_DIM // 2
    d = jnp.arange(LANES) % HEAD_DIM
    rotary = d < ROPE_DIM
    inv_freq = ROPE_THETA ** (-(2 * (d % half)).astype(F32) / ROPE_DIM)
    ang = pos.astype(F32)[:, None] * jnp.where(rotary, inv_freq, 0.0)[None, :]
    cos, sin = jnp.cos(ang), jnp.sin(ang)
    first = (d < half)[None, :]
    return cos, jnp.where(first, 0.0, sin), jnp.where(first, -sin, 0.0)


def _lambda(lq1_ref, lk1_ref, lq2_ref, lk2_ref, lam_init):
    e1 = jnp.exp(jnp.sum(lq1_ref[...] * lk1_ref[...], axis=1, keepdims=True))
    e2 = jnp.exp(jnp.sum(lq2_ref[...] * lk2_ref[...], axis=1, keepdims=True))
    return e1 - e2 + lam_init


def _stack_subheads(q):
    lane = lax.broadcasted_iota(jnp.int32, q.shape, 1)
    zero = jnp.zeros_like(q)
    return jnp.concatenate([jnp.where(lane < HEAD_DIM, q, zero), jnp.where(lane >= HEAD_DIM, q, zero)], axis=0)


def _scores(qs, kb):
    return lax.dot_general(qs, kb, (((1,), (1,)), ((), ())), preferred_element_type=F32)


def _finish_head(o, t, lam, gsub, lam_init):
    d = o[:t] - lam * o[t:]
    return _rms(d, gsub) * (1.0 - lam_init)


def _attn_kernel(lq1_ref, lk1_ref, lq2_ref, lk2_ref, gsub_ref, q_ref, k_ref, v_ref, o_ref,
                 sa_scr, sb_scr, mxa_scr, mxb_scr, m_scr, l_scr, acc_scr, *, t, lam_init):
    i = pl.program_id(1)
    sub = 8
    g = t // sub
    qt = q_ref[0]
    feat = lax.broadcasted_iota(jnp.int32, qt.shape, 0)
    zero = jnp.zeros_like(qt)
    qs = jnp.concatenate([jnp.where(feat < HEAD_DIM, qt, zero), jnp.where(feat >= HEAD_DIM, qt, zero)], axis=1)
    m_scr[...] = jnp.full(m_scr.shape, NEG_INF, F32)
    l_scr[...] = jnp.zeros(l_scr.shape, F32)
    acc_scr[...] = jnp.zeros(acc_scr.shape, F32)

    def qk_stage(j, s_ref, mx_ref):
        s = jnp.dot(k_ref[pl.ds(pl.multiple_of(j * t, t), t), :], qs, preferred_element_type=F32)
        s_ref[...] = s
        mx_ref[...] = jnp.max(s.reshape(g, sub, 2 * t), axis=0)

    def softmax_stage(j, s_ref, mx_ref, masked):
        s3 = s_ref[...].reshape(g, sub, 2 * t)
        if masked:
            shape = (g, sub, 2 * t)
            key = lax.broadcasted_iota(jnp.int32, shape, 0) * sub + lax.broadcasted_iota(jnp.int32, shape, 1)
            col = lax.broadcasted_iota(jnp.int32, shape, 2)
            s3 = jnp.where(key // CHUNK <= jnp.where(col >= t, col - t, col) // CHUNK, s3, NEG_INF)
            m8 = jnp.max(s3, axis=0)
        else:
            m8 = mx_ref[...]
        m_prev = m_scr[...]
        m_new = jnp.maximum(m_prev, jnp.max(m8, axis=0, keepdims=True))
        alpha = jnp.exp2(m_prev - m_new)
        p3 = jnp.exp2(s3 - jnp.broadcast_to(m_new, (sub, 2 * t))[None])
        l_scr[...] = alpha * l_scr[...] + jnp.sum(p3, axis=0)
        pv = jnp.dot(v_ref[j], p3.reshape(t, 2 * t).astype(BF16), preferred_element_type=F32)
        acc_scr[...] = alpha * acc_scr[...] + pv
        m_scr[...] = m_new

    qk_stage(0, sa_scr, mxa_scr)

    def body(jj, carry):
        j = 2 * jj
        qk_stage(j + 1, sb_scr, mxb_scr)
        softmax_stage(j, sa_scr, mxa_scr, False)
        qk_stage(j + 2, sa_scr, mxa_scr)
        softmax_stage(j + 1, sb_scr, mxb_scr, False)
        return carry

    lax.fori_loop(0, i // 2, body, 0)

    @pl.when(i % 2 == 1)
    def _():
        qk_stage(i, sb_scr, mxb_scr)
        softmax_stage(i - 1, sa_scr, mxa_scr, False)
        softmax_stage(i, sb_scr, mxb_scr, True)

    @pl.when(i % 2 == 0)
    def _():
        softmax_stage(i, sa_scr, mxa_scr, True)

    lam = _lambda(lq1_ref, lk1_ref, lq2_ref, lk2_ref, lam_init)
    o = acc_scr[...] / jnp.sum(l_scr[...], axis=0, keepdims=True)
    d = (o[:, :t] - lam * o[:, t:]).T
    o_ref[...] = (_rms(d, gsub_ref[...]) * (1.0 - lam_init)).astype(o_ref.dtype)


def _prompt_attention(lams, gsub, qt, k, vt, *, lam_init):
    nt, _, t = qt.shape
    s = nt * t
    vec_spec = pl.BlockSpec((1, HEAD_DIM), lambda h, i: (0, 0))
    return pl.pallas_call(
        functools.partial(_attn_kernel, t=t, lam_init=lam_init),
        grid=(N_HEADS, nt),
        in_specs=[vec_spec] * 4 + [
            pl.BlockSpec((1, V_DIM), lambda h, i: (0, 0)),
            pl.BlockSpec((1, V_DIM, t), lambda h, i: (i, h, 0)),
            pl.BlockSpec((s, V_DIM), lambda h, i: (0, h)),
            pl.BlockSpec((nt, V_DIM, t), lambda h, i: (0, h, 0)),
        ],
        out_specs=pl.BlockSpec((t, V_DIM), lambda h, i: (i, h)),
        out_shape=jax.ShapeDtypeStruct((s, N_HEADS * V_DIM), BF16),
        scratch_shapes=[pltpu.VMEM((t, 2 * t), F32), pltpu.VMEM((t, 2 * t), F32),
                        pltpu.VMEM((8, 2 * t), F32), pltpu.VMEM((8, 2 * t), F32),
                        pltpu.VMEM((1, 2 * t), F32), pltpu.VMEM((8, 2 * t), F32),
                        pltpu.VMEM((V_DIM, 2 * t), F32)],
        compiler_params=_params(2),
        name="prompt_attn",
    )(*lams, gsub, qt, k, vt)


def _sample_attn_kernel(lq1_ref, lk1_ref, lq2_ref, lk2_ref, gsub_ref, q_ref, kn_ref, vn_ref,
                        ck_ref, cv_ref, o_ref, *, heads, t, lam_init):
    lam = _lambda(lq1_ref, lk1_ref, lq2_ref, lk2_ref, lam_init)
    for hd in range(heads):
        sl = slice(hd * V_DIM, (hd + 1) * V_DIM)
        qs = _stack_subheads(q_ref[:, sl])
        head_rows = pl.ds(hd, ck_ref.shape[1] // heads, stride=heads)
        s_c = _scores(qs, ck_ref[0, head_rows, :].astype(BF16))
        s_n = _scores(qs, kn_ref[:, sl])
        m = jnp.maximum(jnp.max(s_c, axis=1, keepdims=True), jnp.max(s_n, axis=1, keepdims=True))
        p_c = jnp.exp2(s_c - m)
        p_n = jnp.exp2(s_n - m)
        l = jnp.sum(p_c, axis=1, keepdims=True) + jnp.sum(p_n, axis=1, keepdims=True)
        acc = jnp.dot(p_c.astype(BF16), cv_ref[0, head_rows, :].astype(BF16), preferred_element_type=F32)
        acc = acc + jnp.dot(p_n.astype(BF16), vn_ref[:, sl], preferred_element_type=F32)
        o_ref[:, sl] = _finish_head(acc / l, t, lam, gsub_ref[...], lam_init).astype(o_ref.dtype)


def _sample_attention(lams, gsub, q, k_new, v_new, cache_k, cache_v, *, lam_init):
    nb, rows, _ = cache_k.shape
    heads = N_HEADS
    t = q.shape[0] // nb
    vec_spec = pl.BlockSpec((1, HEAD_DIM), lambda b: (0, 0))
    new_spec = pl.BlockSpec((t, heads * V_DIM), lambda b: (b, 0))
    cache_spec = pl.BlockSpec((1, rows, V_DIM), lambda b: (b, 0, 0))
    return pl.pallas_call(
        functools.partial(_sample_attn_kernel, heads=heads, t=t, lam_init=lam_init),
        grid=(nb,),
        in_specs=[vec_spec] * 4 + [pl.BlockSpec((1, V_DIM), lambda b: (0, 0)),
                                   new_spec, new_spec, new_spec, cache_spec, cache_spec],
        out_specs=new_spec,
        out_shape=jax.ShapeDtypeStruct(q.shape, BF16),
        compiler_params=_params(1),
        name="sample_attn",
    )(*lams, gsub, q, k_new, v_new, cache_k, cache_v)


def _merge_kernel(h_ref, attn_ref, u_ref, vn_ref, ga_ref, gb_ref, ws_ref, bs_ref,
                  wpa_ref, wpb_ref, wo_ref, gpost_ref, o_ref, sgu_scr, *, tm, chunk):
    r = lax.broadcasted_iota(jnp.int32, (chunk, chunk), 0)
    c = lax.broadcasted_iota(jnp.int32, (chunk, chunk), 1)
    for g in range(N_GROUPS):
        w = jnp.where(c <= r, ws_ref[g], 0.0).astype(BF16)
        bias = bs_ref[g]
        gl = slice(g * GROUP_DIM, (g + 1) * GROUP_DIM)
        for ci in range(tm // chunk):
            rs = slice(ci * chunk, (ci + 1) * chunk)
            mixed = jnp.dot(w, vn_ref[rs, gl].astype(BF16), preferred_element_type=F32) + bias
            sgu_scr[rs, gl] = (u_ref[rs, gl].astype(F32) * mixed).astype(BF16)
    pa = jnp.dot(attn_ref[...], wpa_ref[...], preferred_element_type=F32)
    pb = jnp.dot(sgu_scr[...], wpb_ref[...], preferred_element_type=F32)
    merged = ga_ref[...].astype(F32) * pa + gb_ref[...].astype(F32) * pb
    mix = jnp.dot(merged.astype(BF16), wo_ref[...], preferred_element_type=F32)
    o_ref[...] = h_ref[...] + _rms(mix, gpost_ref[...])


def _merge(h, attn, u, vn, ga, gb, ws, bs, wpa, wpb, wo, gpost, *, tm, chunk):
    rows = h.shape[0]
    row_spec = pl.BlockSpec((tm, D_MODEL), lambda i: (i, 0))
    return pl.pallas_call(
        functools.partial(_merge_kernel, tm=tm, chunk=chunk),
        grid=(rows // tm,),
        in_specs=[row_spec] * 6 + [_resident(a.shape) for a in (ws, bs, wpa, wpb, wo, gpost)],
        out_specs=row_spec,
        out_shape=jax.ShapeDtypeStruct((rows, D_MODEL), F32),
        scratch_shapes=[pltpu.VMEM((tm, D_B), BF16)],
        compiler_params=_params(1),
        name="merge",
    )(h, attn, u, vn, ga, gb, ws, bs, wpa, wpb, wo, gpost)


def _row_tile(rows, target):
    tm = min(rows, target)
    assert rows % tm == 0, (rows, tm)
    return tm


def kernel(x_prompt, x_sample, cache_k, cache_v, ln_ffn1_pre, w_ffn1_in, w_ffn1_out, ln_ffn1_post, ln_mix_pre, w_in, lambda_q1, lambda_k1, lambda_q2, lambda_k2, ln_subln, ln_sgu, w_spatial, b_spatial, w_proj_a, w_proj_b, w_out, ln_mix_post, ln_ffn2_pre, w_ffn2_in, w_ffn2_out, ln_ffn2_post):
    depth = w_in.shape[0]
    assert depth == 1, "single-layer trunk"
    batch, seq, _ = x_prompt.shape
    dec_batch, dec_seq, _ = x_sample.shape
    past = cache_k.shape[2]
    assert batch == 1 and seq % GMLP_CHUNK == 0 and dec_seq <= GMLP_CHUNK
    lam_init = 0.8 - 0.6 * math.exp(-0.3 * 0)

    row = lambda g: g[0][None, :]
    ffn1 = (row(ln_ffn1_pre), w_ffn1_in[0, :, :D_FF].astype(BF16), w_ffn1_in[0, :, D_FF:].astype(BF16),
            w_ffn1_out[0].astype(BF16), row(ln_ffn1_post))
    ffn2 = (row(ln_ffn2_pre), w_ffn2_in[0, :, :D_FF].astype(BF16), w_ffn2_in[0, :, D_FF:].astype(BF16),
            w_ffn2_out[0].astype(BF16), row(ln_ffn2_post))
    w_in_b = w_in[0].astype(BF16)
    wpa, wpb, wo = w_proj_a[0].astype(BF16), w_proj_b[0].astype(BF16), w_out[0].astype(BF16)
    lams = (lambda_q1, lambda_k1, lambda_q2, lambda_k2)
    gsub = row(ln_subln)

    def layer(x, pos, chunk, attend, tm, vn_dtype, transposed_qv):
        h = _ffn(x, *ffn1, tm=tm)
        q, kf, kb, vf, vb, u, vn, ga, gb = _mix_in(
            h, row(ln_mix_pre), w_in_b, row(ln_sgu), _rope_tables(pos), tm=tm, vn_dtype=vn_dtype,
            transposed_qv=transposed_qv)
        attn = attend(q, kb, vb)
        ws = w_spatial[0, :, :chunk, :chunk]
        bs = b_spatial[0, :, :chunk, None]
        h = _merge(h, attn, u, vn, ga, gb, ws, bs, wpa, wpb, wo, row(ln_mix_post), tm=tm, chunk=chunk)
        y = _ffn(h, *ffn2, tm=tm)
        return y, kf, vf, vn

    yp, kp, vp, _ = layer(
        x_prompt.reshape(seq, D_MODEL), jnp.arange(seq), GMLP_CHUNK,
        lambda qt, k, vt: _prompt_attention(lams, gsub, qt, k, vt, lam_init=lam_init),
        _row_tile(seq, 512), BF16, True)

    rows_s = dec_batch * dec_seq
    ys, ks, vs, gs = layer(
        x_sample.reshape(rows_s, D_MODEL), jnp.tile(past + jnp.arange(dec_seq), dec_batch), dec_seq,
        lambda q, k, v: _sample_attention(
            lams, gsub, q, k, v, cache_k[0].reshape(dec_batch, past * N_HEADS, V_DIM),
            cache_v[0].reshape(dec_batch, past * N_HEADS, V_DIM), lam_init=lam_init),
        _row_tile(rows_s, 256), F32, False)

    return (yp.reshape(batch, seq, D_MODEL),
            ys.reshape(dec_batch, dec_seq, D_MODEL),
            kp.reshape(1, batch, seq, N_HEADS, V_DIM),
            vp.reshape(1, batch, seq, N_HEADS, V_DIM),
            ks.reshape(1, dec_batch, dec_seq, N_HEADS, V_DIM),
            vs.reshape(1, dec_batch, dec_seq, N_HEADS, V_DIM),
            gs.reshape(1, dec_batch, dec_seq, D_B))
```

```python
import functools
import math

import jax
import jax.numpy as jnp
from jax import lax
from jax.experimental import pallas as pl
from jax.experimental.pallas import tpu as pltpu

F32 = jnp.float32
BF16 = jnp.bfloat16

D_MODEL = 1024
N_HEADS = 8
HEAD_DIM = 64
V_DIM = 2 * HEAD_DIM
ROPE_DIM = HEAD_DIM // 4
ROPE_THETA = 500000.0
CHUNK = 64
GMLP_CHUNK = 128
N_GROUPS = 4
D_B = 1024
GROUP_DIM = D_B // N_GROUPS
D_FF = 2816
EPS = 1e-6
SCALE = HEAD_DIM ** -0.5
LOG2E = 1.4426950408889634
NEG_INF = -1e30
N_SEG = 7

LANES = 128
VMEM_LIMIT_BYTES = 60 * 1024 * 1024


def _rms(x, g):
    return x * lax.rsqrt(jnp.mean(x * x, axis=-1, keepdims=True) + EPS) * g


def _sigmoid(x):
    return 1.0 / (1.0 + jnp.exp(-x))


def _resident(shape):
    zeros = (0,) * len(shape)
    return pl.BlockSpec(shape, lambda *_: zeros, pipeline_mode=pl.Buffered(1))


def _params(n_axes, flags=None):
    return pltpu.CompilerParams(
        dimension_semantics=("arbitrary",) * n_axes, vmem_limit_bytes=VMEM_LIMIT_BYTES, flags=flags)


def _ffn_kernel(x_ref, gpre_ref, wa_ref, wb_ref, wo_ref, gpost_ref, o_ref):
    x = x_ref[...]
    n = _rms(x, gpre_ref[...]).astype(BF16)
    a = jnp.dot(n, wa_ref[...], preferred_element_type=F32)
    b = jnp.dot(n, wb_ref[...], preferred_element_type=F32)
    g = (a * _sigmoid(a) * b).astype(BF16)
    f = jnp.dot(g, wo_ref[...], preferred_element_type=F32)
    o_ref[...] = x + 0.5 * _rms(f, gpost_ref[...])


def _ffn(x, gpre, wa, wb, wo, gpost, *, tm):
    rows = x.shape[0]
    row_spec = pl.BlockSpec((tm, D_MODEL), lambda i: (i, 0))
    return pl.pallas_call(
        _ffn_kernel,
        grid=(rows // tm,),
        in_specs=[row_spec, _resident(gpre.shape), _resident(wa.shape), _resident(wb.shape),
                  _resident(wo.shape), _resident(gpost.shape)],
        out_specs=row_spec,
        out_shape=jax.ShapeDtypeStruct((rows, D_MODEL), F32),
        compiler_params=_params(1),
        name="ffn",
    )(x, gpre, wa, wb, wo, gpost)


def _gelu(x):
    return jax.nn.gelu(x, approximate=True)


def _mix_in_kernel(h_ref, gpre_ref, w_ref, gsgu_ref, rc_ref, rs1_ref, rs2_ref,
                   q_ref, kf_ref, kb_ref, vf_ref, vb_ref, u_ref, vn_ref, ga_ref, gb_ref, *, transposed_qv):
    n = _rms(h_ref[...], gpre_ref[...]).astype(BF16)

    def put_qv(ref, hd, x):
        if transposed_qv:
            ref[0, hd * V_DIM:(hd + 1) * V_DIM, :] = x.T.astype(BF16)
        else:
            ref[:, hd * V_DIM:(hd + 1) * V_DIM] = x.astype(BF16)

    def seg(j):
        return jnp.dot(n, w_ref[:, j * D_MODEL:(j + 1) * D_MODEL], preferred_element_type=F32)

    rc, rs1, rs2 = rc_ref[...], rs1_ref[...], rs2_ref[...]

    def rope(x):
        return x * rc + pltpu.roll(x, ROPE_DIM // 2, 1) * rs1 + pltpu.roll(x, LANES - ROPE_DIM // 2, 1) * rs2

    q = seg(0)
    for hd in range(N_HEADS):
        sl = slice(hd * V_DIM, (hd + 1) * V_DIM)
        put_qv(q_ref, hd, rope(q[:, sl]) * (SCALE * LOG2E))
    k = seg(1)
    for hd in range(N_HEADS):
        sl = slice(hd * V_DIM, (hd + 1) * V_DIM)
        kr = rope(k[:, sl])
        kf_ref[:, sl] = kr
        kb_ref[:, sl] = kr.astype(BF16)
    v = seg(2)
    vf_ref[...] = v
    for hd in range(N_HEADS):
        put_qv(vb_ref, hd, v[:, hd * V_DIM:(hd + 1) * V_DIM])
    u_ref[...] = _gelu(seg(3)).astype(u_ref.dtype)
    vn_ref[...] = _rms(_gelu(seg(4)), gsgu_ref[...]).astype(vn_ref.dtype)
    ga_ref[...] = _sigmoid(seg(5)).astype(ga_ref.dtype)
    gb_ref[...] = _sigmoid(seg(6)).astype(gb_ref.dtype)


def _mix_in(h, gpre, w_in, gsgu, rope_tabs, *, tm, vn_dtype, transposed_qv):
    rows = h.shape[0]
    row_spec = pl.BlockSpec((tm, D_MODEL), lambda i: (i, 0))
    tab_spec = pl.BlockSpec((tm, LANES), lambda i: (i, 0))
    row_out = lambda dt: (row_spec, jax.ShapeDtypeStruct((rows, D_MODEL), dt))
    if transposed_qv:
        qv_out = (pl.BlockSpec((1, D_MODEL, tm), lambda i: (i, 0, 0)),
                  jax.ShapeDtypeStruct((rows // tm, D_MODEL, tm), BF16))
    else:
        qv_out = row_out(BF16)
    outs = [qv_out, row_out(F32), row_out(BF16), row_out(F32), qv_out, row_out(BF16), row_out(vn_dtype),
            row_out(BF16), row_out(BF16)]
    return pl.pallas_call(
        functools.partial(_mix_in_kernel, transposed_qv=transposed_qv),
        grid=(rows // tm,),
        in_specs=[row_spec, _resident(gpre.shape), _resident(w_in.shape), _resident(gsgu.shape),
                  tab_spec, tab_spec, tab_spec],
        out_specs=[o[0] for o in outs],
        out_shape=[o[1] for o in outs],
        compiler_params=_params(1),
        name="mix_in",
    )(h, gpre, w_in, gsgu, *rope_tabs)


def _rope_tables(pos):
    half = ROPE_DIM // 2
    d = jnp.arange(LANES) % HEAD_DIM
    rotary = d < ROPE_DIM
    inv_freq = ROPE_THETA ** (-(2 * (d % half)).astype(F32) / ROPE_DIM)
    ang = pos.astype(F32)[:, None] * jnp.where(rotary, inv_freq, 0.0)[None, :]
    cos, sin = jnp.cos(ang), jnp.sin(ang)
    first = (d < half)[None, :]
    return cos, jnp.where(first, 0.0, sin), jnp.where(first, -sin, 0.0)


def _lambda(lq1_ref, lk1_ref, lq2_ref, lk2_ref, lam_init):
    e1 = jnp.exp(jnp.sum(lq1_ref[...] * lk1_ref[...], axis=1, keepdims=True))
    e2 = jnp.exp(jnp.sum(lq2_ref[...] * lk2_ref[...], axis=1, keepdims=True))
    return e1 - e2 + lam_init


def _stack_subheads(q):
    lane = lax.broadcasted_iota(jnp.int32, q.shape, 1)
    zero = jnp.zeros_like(q)
    return jnp.concatenate([jnp.where(lane < HEAD_DIM, q, zero), jnp.where(lane >= HEAD_DIM, q, zero)], axis=0)


def _scores(qs, kb):
    return lax.dot_general(qs, kb, (((1,), (1,)), ((), ())), preferred_element_type=F32)


def _finish_head(o, t, lam, gsub, lam_init):
    d = o[:t] - lam * o[t:]
    return _rms(d, gsub) * (1.0 - lam_init)


def _attn_kernel(lq1_ref, lk1_ref, lq2_ref, lk2_ref, gsub_ref, q_ref, k_ref, v_ref, o_ref,
                 sa_scr, sb_scr, mxa_scr, mxb_scr, m_scr, l_scr, acc_scr, *, t, lam_init):
    i = pl.program_id(1)
    sub = 8
    g = t // sub
    qt = q_ref[0]
    feat = lax.broadcasted_iota(jnp.int32, qt.shape, 0)
    zero = jnp.zeros_like(qt)
    qs = jnp.concatenate([jnp.where(feat < HEAD_DIM, qt, zero), jnp.where(feat >= HEAD_DIM, qt, zero)], axis=1)
    m_scr[...] = jnp.full(m_scr.shape, NEG_INF, F32)
    l_scr[...] = jnp.zeros(l_scr.shape, F32)
    acc_scr[...] = jnp.zeros(acc_scr.shape, F32)

    def qk_stage(j, s_ref, mx_ref):
        s = jnp.dot(k_ref[pl.ds(pl.multiple_of(j * t, t), t), :], qs, preferred_element_type=F32)
        s_ref[...] = s
        mx_ref[...] = jnp.max(s.reshape(g, sub, 2 * t), axis=0)

    def softmax_stage(j, s_ref, mx_ref, masked):
        s3 = s_ref[...].reshape(g, sub, 2 * t)
        if masked:
            shape = (g, sub, 2 * t)
            key = lax.broadcasted_iota(jnp.int32, shape, 0) * sub + lax.broadcasted_iota(jnp.int32, shape, 1)
            col = lax.broadcasted_iota(jnp.int32, shape, 2)
            s3 = jnp.where(key // CHUNK <= jnp.where(col >= t, col - t, col) // CHUNK, s3, NEG_INF)
            m8 = jnp.max(s3, axis=0)
        else:
            m8 = mx_ref[...]
        m_prev = m_scr[...]
        m_new = jnp.maximum(m_prev, jnp.max(m8, axis=0, keepdims=True))
        alpha = jnp.exp2(m_prev - m_new)
        p3 = jnp.exp2(s3 - jnp.broadcast_to(m_new, (sub, 2 * t))[None])
        l_scr[...] = alpha * l_scr[...] + jnp.sum(p3, axis=0)
        pv = jnp.dot(v_ref[j], p3.reshape(t, 2 * t).astype(BF16), preferred_element_type=F32)
        acc_scr[...] = alpha * acc_scr[...] + pv
        m_scr[...] = m_new

    qk_stage(0, sa_scr, mxa_scr)

    def pair(j):
        qk_stage(j + 1, sb_scr, mxb_scr)
        softmax_stage(j, sa_scr, mxa_scr, False)
        qk_stage(j + 2, sa_scr, mxa_scr)
        softmax_stage(j + 1, sb_scr, mxb_scr, False)

    def body(jj, carry):
        pair(4 * jj)
        pair(4 * jj + 2)
        return carry

    lax.fori_loop(0, i // 4, body, 0)

    @pl.when(i % 4 >= 2)
    def _():
        pair(4 * (i // 4))

    @pl.when(i % 2 == 1)
    def _():
        qk_stage(i, sb_scr, mxb_scr)
        softmax_stage(i - 1, sa_scr, mxa_scr, False)
        softmax_stage(i, sb_scr, mxb_scr, True)

    @pl.when(i % 2 == 0)
    def _():
        softmax_stage(i, sa_scr, mxa_scr, True)

    lam = _lambda(lq1_ref, lk1_ref, lq2_ref, lk2_ref, lam_init)
    o = acc_scr[...] / jnp.sum(l_scr[...], axis=0, keepdims=True)
    d = (o[:, :t] - lam * o[:, t:]).T
    o_ref[...] = (_rms(d, gsub_ref[...]) * (1.0 - lam_init)).astype(o_ref.dtype)


def _prompt_attention(lams, gsub, qt, k, vt, *, lam_init):
    nt, _, t = qt.shape
    s = nt * t
    vec_spec = pl.BlockSpec((1, HEAD_DIM), lambda h, i: (0, 0))
    return pl.pallas_call(
        functools.partial(_attn_kernel, t=t, lam_init=lam_init),
        grid=(N_HEADS, nt),
        in_specs=[vec_spec] * 4 + [
            pl.BlockSpec((1, V_DIM), lambda h, i: (0, 0)),
            pl.BlockSpec((1, V_DIM, t), lambda h, i: (i, h, 0)),
            pl.BlockSpec((s, V_DIM), lambda h, i: (0, h)),
            pl.BlockSpec((nt, V_DIM, t), lambda h, i: (0, h, 0)),
        ],
        out_specs=pl.BlockSpec((t, V_DIM), lambda h, i: (i, h)),
        out_shape=jax.ShapeDtypeStruct((s, N_HEADS * V_DIM), BF16),
        scratch_shapes=[pltpu.VMEM((t, 2 * t), F32), pltpu.VMEM((t, 2 * t), F32),
                        pltpu.VMEM((8, 2 * t), F32), pltpu.VMEM((8, 2 * t), F32),
                        pltpu.VMEM((1, 2 * t), F32), pltpu.VMEM((8, 2 * t), F32),
                        pltpu.VMEM((V_DIM, 2 * t), F32)],
        compiler_params=_params(2),
        name="prompt_attn",
    )(*lams, gsub, qt, k, vt)


def _sample_attn_kernel(lq1_ref, lk1_ref, lq2_ref, lk2_ref, gsub_ref, q_ref, kn_ref, vn_ref,
                        ck_ref, cv_ref, o_ref, *, heads, t, lam_init):
    lam = _lambda(lq1_ref, lk1_ref, lq2_ref, lk2_ref, lam_init)
    for hd in range(heads):
        sl = slice(hd * V_DIM, (hd + 1) * V_DIM)
        qs = _stack_subheads(q_ref[:, sl])
        head_rows = pl.ds(hd, ck_ref.shape[1] // heads, stride=heads)
        s_c = _scores(qs, ck_ref[0, head_rows, :].astype(BF16))
        s_n = _scores(qs, kn_ref[:, sl])
        m = jnp.maximum(jnp.max(s_c, axis=1, keepdims=True), jnp.max(s_n, axis=1, keepdims=True))
        p_c = jnp.exp2(s_c - m)
        p_n = jnp.exp2(s_n - m)
        l = jnp.sum(p_c, axis=1, keepdims=True) + jnp.sum(p_n, axis=1, keepdims=True)
        acc = jnp.dot(p_c.astype(BF16), cv_ref[0, head_rows, :].astype(BF16), preferred_element_type=F32)
        acc = acc + jnp.dot(p_n.astype(BF16), vn_ref[:, sl], preferred_element_type=F32)
        o_ref[:, sl] = _finish_head(acc / l, t, lam, gsub_ref[...], lam_init).astype(o_ref.dtype)


def _sample_attention(lams, gsub, q, k_new, v_new, cache_k, cache_v, *, lam_init):
    nb, rows, _ = cache_k.shape
    heads = N_HEADS
    t = q.shape[0] // nb
    vec_spec = pl.BlockSpec((1, HEAD_DIM), lambda b: (0, 0))
    new_spec = pl.BlockSpec((t, heads * V_DIM), lambda b: (b, 0))
    cache_spec = pl.BlockSpec((1, rows, V_DIM), lambda b: (b, 0, 0))
    return pl.pallas_call(
        functools.partial(_sample_attn_kernel, heads=heads, t=t, lam_init=lam_init),
        grid=(nb,),
        in_specs=[vec_spec] * 4 + [pl.BlockSpec((1, V_DIM), lambda b: (0, 0)),
                                   new_spec, new_spec, new_spec, cache_spec, cache_spec],
        out_specs=new_spec,
        out_shape=jax.ShapeDtypeStruct(q.shape, BF16),
        compiler_params=_params(1),
        name="sample_attn",
    )(*lams, gsub, q, k_new, v_new, cache_k, cache_v)


def _merge_kernel(h_ref, attn_ref, u_ref, vn_ref, ga_ref, gb_ref, ws_ref, bs_ref,
                  wpa_ref, wpb_ref, wo_ref, gpost_ref, o_ref, sgu_scr, *, tm, chunk):
    r = lax.broadcasted_iota(jnp.int32, (chunk, chunk), 0)
    c = lax.broadcasted_iota(jnp.int32, (chunk, chunk), 1)
    for g in range(N_GROUPS):
        w = jnp.where(c <= r, ws_ref[g], 0.0).astype(BF16)
        bias = bs_ref[g]
        gl = slice(g * GROUP_DIM, (g + 1) * GROUP_DIM)
        for ci in range(tm // chunk):
            rs = slice(ci * chunk, (ci + 1) * chunk)
            mixed = jnp.dot(w, vn_ref[rs, gl].astype(BF16), preferred_element_type=F32) + bias
            sgu_scr[rs, gl] = (u_ref[rs, gl].astype(F32) * mixed).astype(BF16)
    pa = jnp.dot(attn_ref[...], wpa_ref[...], preferred_element_type=F32)
    pb = jnp.dot(sgu_scr[...], wpb_ref[...], preferred_element_type=F32)
    merged = ga_ref[...].astype(F32) * pa + gb_ref[...].astype(F32) * pb
    mix = jnp.dot(merged.astype(BF16), wo_ref[...], preferred_element_type=F32)
    o_ref[...] = h_ref[...] + _rms(mix, gpost_ref[...])


def _merge(h, attn, u, vn, ga, gb, ws, bs, wpa, wpb, wo, gpost, *, tm, chunk):
    rows = h.shape[0]
    row_spec = pl.BlockSpec((tm, D_MODEL), lambda i: (i, 0))
    return pl.pallas_call(
        functools.partial(_merge_kernel, tm=tm, chunk=chunk),
        grid=(rows // tm,),
        in_specs=[row_spec] * 6 + [_resident(a.shape) for a in (ws, bs, wpa, wpb, wo, gpost)],
        out_specs=row_spec,
        out_shape=jax.ShapeDtypeStruct((rows, D_MODEL), F32),
        scratch_shapes=[pltpu.VMEM((tm, D_B), BF16)],
        compiler_params=_params(1),
        name="merge",
    )(h, attn, u, vn, ga, gb, ws, bs, wpa, wpb, wo, gpost)


def _row_tile(rows, target):
    tm = min(rows, target)
    assert rows % tm == 0, (rows, tm)
    return tm


def kernel(x_prompt, x_sample, cache_k, cache_v, ln_ffn1_pre, w_ffn1_in, w_ffn1_out, ln_ffn1_post, ln_mix_pre, w_in, lambda_q1, lambda_k1, lambda_q2, lambda_k2, ln_subln, ln_sgu, w_spatial, b_spatial, w_proj_a, w_proj_b, w_out, ln_mix_post, ln_ffn2_pre, w_ffn2_in, w_ffn2_out, ln_ffn2_post):
    depth = w_in.shape[0]
    assert depth == 1, "single-layer trunk"
    batch, seq, _ = x_prompt.shape
    dec_batch, dec_seq, _ = x_sample.shape
    past = cache_k.shape[2]
    assert batch == 1 and seq % GMLP_CHUNK == 0 and dec_seq <= GMLP_CHUNK
    lam_init = 0.8 - 0.6 * math.exp(-0.3 * 0)

    row = lambda g: g[0][None, :]
    ffn1 = (row(ln_ffn1_pre), w_ffn1_in[0, :, :D_FF].astype(BF16), w_ffn1_in[0, :, D_FF:].astype(BF16),
            w_ffn1_out[0].astype(BF16), row(ln_ffn1_post))
    ffn2 = (row(ln_ffn2_pre), w_ffn2_in[0, :, :D_FF].astype(BF16), w_ffn2_in[0, :, D_FF:].astype(BF16),
            w_ffn2_out[0].astype(BF16), row(ln_ffn2_post))
    w_in_b = w_in[0].astype(BF16)
    wpa, wpb, wo = w_proj_a[0].astype(BF16), w_proj_b[0].astype(BF16), w_out[0].astype(BF16)
    lams = (lambda_q1, lambda_k1, lambda_q2, lambda_k2)
    gsub = row(ln_subln)

    def layer(x, pos, chunk, attend, tm, vn_dtype, transposed_qv):
        h = _ffn(x, *ffn1, tm=tm)
        q, kf, kb, vf, vb, u, vn, ga, gb = _mix_in(
            h, row(ln_mix_pre), w_in_b, row(ln_sgu), _rope_tables(pos), tm=tm, vn_dtype=vn_dtype,
            transposed_qv=transposed_qv)
        attn = attend(q, kb, vb)
        ws = w_spatial[0, :, :chunk, :chunk]
        bs = b_spatial[0, :, :chunk, None]
        h = _merge(h, attn, u, vn, ga, gb, ws, bs, wpa, wpb, wo, row(ln_mix_post), tm=tm, chunk=chunk)
        y = _ffn(h, *ffn2, tm=tm)
        return y, kf, vf, vn

    yp, kp, vp, _ = layer(
        x_prompt.reshape(seq, D_MODEL), jnp.arange(seq), GMLP_CHUNK,
        lambda qt, k, vt: _prompt_attention(lams, gsub, qt, k, vt, lam_init=lam_init),
        _row_tile(seq, 512), BF16, True)

    rows_s = dec_batch * dec_seq
    ys, ks, vs, gs = layer(
        x_sample.reshape(rows_s, D_MODEL), jnp.tile(past + jnp.arange(dec_seq), dec_batch), dec_seq,
        lambda q, k, v: _sample_attention(
            lams, gsub, q, k, v, cache_k[0].reshape(dec_batch, past * N_HEADS, V_DIM),
            cache_v[0].reshape(dec_batch, past * N_HEADS, V_DIM), lam_init=lam_init),
        _row_tile(rows_s, 256), F32, False)

    return (yp.reshape(batch, seq, D_MODEL),
            ys.reshape(dec_batch, dec_seq, D_MODEL),
            kp.reshape(1, batch, seq, N_HEADS, V_DIM),
            vp.reshape(1, batch, seq, N_HEADS, V_DIM),
            ks.reshape(1, dec_batch, dec_seq, N_HEADS, V_DIM),
            vs.reshape(1, dec_batch, dec_seq, N_HEADS, V_DIM),
            gs.reshape(1, dec_batch, dec_seq, D_B))
```

```python
import functools
import math

import jax
import jax.numpy as jnp
from jax import lax
from jax.experimental import pallas as pl
from jax.experimental.pallas import tpu as pltpu

F32 = jnp.float32
BF16 = jnp.bfloat16

D_MODEL = 1024
N_HEADS = 8
HEAD_DIM = 64
V_DIM = 2 * HEAD_DIM
ROPE_DIM = HEAD_DIM // 4
ROPE_THETA = 500000.0
CHUNK = 64
GMLP_CHUNK = 128
N_GROUPS = 4
D_B = 1024
GROUP_DIM = D_B // N_GROUPS
D_FF = 2816
EPS = 1e-6
SCALE = HEAD_DIM ** -0.5
LOG2E = 1.4426950408889634
NEG_INF = -1e30
N_SEG = 7

LANES = 128
VMEM_LIMIT_BYTES = 60 * 1024 * 1024


def _rms(x, g):
    return x * lax.rsqrt(jnp.mean(x * x, axis=-1, keepdims=True) + EPS) * g


def _sigmoid(x):
    return 1.0 / (1.0 + jnp.exp(-x))


def _resident(shape):
    zeros = (0,) * len(shape)
    return pl.BlockSpec(shape, lambda *_: zeros, pipeline_mode=pl.Buffered(1))


def _params(n_axes, flags=None):
    return pltpu.CompilerParams(
        dimension_semantics=("arbitrary",) * n_axes, vmem_limit_bytes=VMEM_LIMIT_BYTES, flags=flags)


def _ffn_kernel(x_ref, gpre_ref, wa_ref, wb_ref, wo_ref, gpost_ref, o_ref):
    x = x_ref[...]
    n = _rms(x, gpre_ref[...]).astype(BF16)
    a = jnp.dot(n, wa_ref[...], preferred_element_type=F32)
    b = jnp.dot(n, wb_ref[...], preferred_element_type=F32)
    g = (a * _sigmoid(a) * b).astype(BF16)
    f = jnp.dot(g, wo_ref[...], preferred_element_type=F32)
    o_ref[...] = x + 0.5 * _rms(f, gpost_ref[...])


def _ffn(x, gpre, wa, wb, wo, gpost, *, tm):
    rows = x.shape[0]
    row_spec = pl.BlockSpec((tm, D_MODEL), lambda i: (i, 0))
    return pl.pallas_call(
        _ffn_kernel,
        grid=(rows // tm,),
        in_specs=[row_spec, _resident(gpre.shape), _resident(wa.shape), _resident(wb.shape),
                  _resident(wo.shape), _resident(gpost.shape)],
        out_specs=row_spec,
        out_shape=jax.ShapeDtypeStruct((rows, D_MODEL), F32),
        compiler_params=_params(1),
        name="ffn",
    )(x, gpre, wa, wb, wo, gpost)


def _gelu(x):
    return jax.nn.gelu(x, approximate=True)


def _mix_in_kernel(h_ref, gpre_ref, w_ref, gsgu_ref, rc_ref, rs1_ref, rs2_ref,
                   q_ref, kf_ref, kb_ref, vf_ref, vb_ref, u_ref, vn_ref, ga_ref, gb_ref, *, transposed_qv):
    n = _rms(h_ref[...], gpre_ref[...]).astype(BF16)

    def put_qv(ref, hd, x):
        if transposed_qv:
            ref[0, hd * V_DIM:(hd + 1) * V_DIM, :] = x.T.astype(BF16)
        else:
            ref[:, hd * V_DIM:(hd + 1) * V_DIM] = x.astype(BF16)

    def seg(j):
        return jnp.dot(n, w_ref[:, j * D_MODEL:(j + 1) * D_MODEL], preferred_element_type=F32)

    rc, rs1, rs2 = rc_ref[...], rs1_ref[...], rs2_ref[...]

    def rope(x):
        return x * rc + pltpu.roll(x, ROPE_DIM // 2, 1) * rs1 + pltpu.roll(x, LANES - ROPE_DIM // 2, 1) * rs2

    q = seg(0)
    for hd in range(N_HEADS):
        sl = slice(hd * V_DIM, (hd + 1) * V_DIM)
        put_qv(q_ref, hd, rope(q[:, sl]) * (SCALE * LOG2E))
    k = seg(1)
    for hd in range(N_HEADS):
        sl = slice(hd * V_DIM, (hd + 1) * V_DIM)
        kr = rope(k[:, sl])
        kf_ref[:, sl] = kr
        kb_ref[:, sl] = kr.astype(BF16)
    v = seg(2)
    vf_ref[...] = v
    for hd in range(N_HEADS):
        put_qv(vb_ref, hd, v[:, hd * V_DIM:(hd + 1) * V_DIM])
    u_ref[...] = _gelu(seg(3)).astype(u_ref.dtype)
    vn_ref[...] = _rms(_gelu(seg(4)), gsgu_ref[...]).astype(vn_ref.dtype)
    ga_ref[...] = _sigmoid(seg(5)).astype(ga_ref.dtype)
    gb_ref[...] = _sigmoid(seg(6)).astype(gb_ref.dtype)


def _mix_in(h, gpre, w_in, gsgu, rope_tabs, *, tm, vn_dtype, transposed_qv):
    rows = h.shape[0]
    row_spec = pl.BlockSpec((tm, D_MODEL), lambda i: (i, 0))
    tab_spec = pl.BlockSpec((tm, LANES), lambda i: (i, 0))
    row_out = lambda dt: (row_spec, jax.ShapeDtypeStruct((rows, D_MODEL), dt))
    if transposed_qv:
        qv_out = (pl.BlockSpec((1, D_MODEL, tm), lambda i: (i, 0, 0)),
                  jax.ShapeDtypeStruct((rows // tm, D_MODEL, tm), BF16))
    else:
        qv_out = row_out(BF16)
    outs = [qv_out, row_out(F32), row_out(BF16), row_out(F32), qv_out, row_out(BF16), row_out(vn_dtype),
            row_out(BF16), row_out(BF16)]
    return pl.pallas_call(
        functools.partial(_mix_in_kernel, transposed_qv=transposed_qv),
        grid=(rows // tm,),
        in_specs=[row_spec, _resident(gpre.shape), _resident(w_in.shape), _resident(gsgu.shape),
                  tab_spec, tab_spec, tab_spec],
        out_specs=[o[0] for o in outs],
        out_shape=[o[1] for o in outs],
        compiler_params=_params(1),
        name="mix_in",
    )(h, gpre, w_in, gsgu, *rope_tabs)


def _rope_tables(pos):
    half = ROPE_DIM // 2
    d = jnp.arange(LANES) % HEAD_DIM
    rotary = d < ROPE_DIM
    inv_freq = ROPE_THETA ** (-(2 * (d % half)).astype(F32) / ROPE_DIM)
    ang = pos.astype(F32)[:, None] * jnp.where(rotary, inv_freq, 0.0)[None, :]
    cos, sin = jnp.cos(ang), jnp.sin(ang)
    first = (d < half)[None, :]
    return cos, jnp.where(first, 0.0, sin), jnp.where(first, -sin, 0.0)


def _lambda(lq1_ref, lk1_ref, lq2_ref, lk2_ref, lam_init):
    e1 = jnp.exp(jnp.sum(lq1_ref[...] * lk1_ref[...], axis=1, keepdims=True))
    e2 = jnp.exp(jnp.sum(lq2_ref[...] * lk2_ref[...], axis=1, keepdims=True))
    return e1 - e2 + lam_init


def _stack_subheads(q):
    lane = lax.broadcasted_iota(jnp.int32, q.shape, 1)
    zero = jnp.zeros_like(q)
    return jnp.concatenate([jnp.where(lane < HEAD_DIM, q, zero), jnp.where(lane >= HEAD_DIM, q, zero)], axis=0)


def _scores(qs, kb):
    return lax.dot_general(qs, kb, (((1,), (1,)), ((), ())), preferred_element_type=F32)


def _finish_head(o, t, lam, gsub, lam_init):
    d = o[:t] - lam * o[t:]
    return _rms(d, gsub) * (1.0 - lam_init)


def _attn_kernel(lq1_ref, lk1_ref, lq2_ref, lk2_ref, gsub_ref, q_ref, k_ref, v_ref, o_ref,
                 sa_scr, sb_scr, mxa_scr, mxb_scr, m_scr, l_scr, acc_scr, *, t, lam_init):
    i = pl.program_id(1)
    sub = 8
    g = t // sub
    qt = q_ref[0]
    feat = lax.broadcasted_iota(jnp.int32, qt.shape, 0)
    zero = jnp.zeros_like(qt)
    qs = jnp.concatenate([jnp.where(feat < HEAD_DIM, qt, zero), jnp.where(feat >= HEAD_DIM, qt, zero)], axis=1)
    m_scr[...] = jnp.full(m_scr.shape, NEG_INF, F32)
    l_scr[...] = jnp.zeros(l_scr.shape, F32)
    acc_scr[...] = jnp.zeros(acc_scr.shape, F32)

    def qk_stage(j, s_ref, mx_ref):
        s = jnp.dot(k_ref[pl.ds(pl.multiple_of(j * t, t), t), :], qs, preferred_element_type=F32)
        s_ref[...] = s
        mx_ref[...] = jnp.max(s.reshape(g, sub, 2 * t), axis=0)

    def softmax_stage(j, s_ref, mx_ref, masked):
        s3 = s_ref[...].reshape(g, sub, 2 * t)
        if masked:
            shape = (g, sub, 2 * t)
            key = lax.broadcasted_iota(jnp.int32, shape, 0) * sub + lax.broadcasted_iota(jnp.int32, shape, 1)
            col = lax.broadcasted_iota(jnp.int32, shape, 2)
            s3 = jnp.where(key // CHUNK <= jnp.where(col >= t, col - t, col) // CHUNK, s3, NEG_INF)
            m8 = jnp.max(s3, axis=0)
        else:
            m8 = mx_ref[...]
        m_prev = m_scr[...]
        m_new = jnp.maximum(m_prev, jnp.max(m8, axis=0, keepdims=True))
        alpha = jnp.exp2(m_prev - m_new)
        p3 = jnp.exp2(s3 - jnp.broadcast_to(m_new, (sub, 2 * t))[None])
        l_scr[...] = alpha * l_scr[...] + jnp.sum(p3, axis=0)
        pv = jnp.dot(v_ref[j], p3.reshape(t, 2 * t).astype(BF16), preferred_element_type=F32)
        acc_scr[...] = alpha * acc_scr[...] + pv
        m_scr[...] = m_new

    qk_stage(0, sa_scr, mxa_scr)

    def pair(j):
        qk_stage(j + 1, sb_scr, mxb_scr)
        softmax_stage(j, sa_scr, mxa_scr, False)
        qk_stage(j + 2, sa_scr, mxa_scr)
        softmax_stage(j + 1, sb_scr, mxb_scr, False)

    def quad(j):
        pair(j)
        pair(j + 2)

    def body(jj, carry):
        quad(8 * jj)
        quad(8 * jj + 4)
        return carry

    lax.fori_loop(0, i // 8, body, 0)

    @pl.when(i % 8 >= 4)
    def _():
        quad(8 * (i // 8))

    @pl.when(i % 4 >= 2)
    def _():
        pair(4 * (i // 4))

    @pl.when(i % 2 == 1)
    def _():
        qk_stage(i, sb_scr, mxb_scr)
        softmax_stage(i - 1, sa_scr, mxa_scr, False)
        softmax_stage(i, sb_scr, mxb_scr, True)

    @pl.when(i % 2 == 0)
    def _():
        softmax_stage(i, sa_scr, mxa_scr, True)

    lam = _lambda(lq1_ref, lk1_ref, lq2_ref, lk2_ref, lam_init)
    o = acc_scr[...] / jnp.sum(l_scr[...], axis=0, keepdims=True)
    d = (o[:, :t] - lam * o[:, t:]).T
    o_ref[...] = (_rms(d, gsub_ref[...]) * (1.0 - lam_init)).astype(o_ref.dtype)


def _prompt_attention(lams, gsub, qt, k, vt, *, lam_init):
    nt, _, t = qt.shape
    s = nt * t
    vec_spec = pl.BlockSpec((1, HEAD_DIM), lambda h, i: (0, 0))
    return pl.pallas_call(
        functools.partial(_attn_kernel, t=t, lam_init=lam_init),
        grid=(N_HEADS, nt),
        in_specs=[vec_spec] * 4 + [
            pl.BlockSpec((1, V_DIM), lambda h, i: (0, 0)),
            pl.BlockSpec((1, V_DIM, t), lambda h, i: (i, h, 0)),
            pl.BlockSpec((s, V_DIM), lambda h, i: (0, h)),
            pl.BlockSpec((nt, V_DIM, t), lambda h, i: (0, h, 0)),
        ],
        out_specs=pl.BlockSpec((t, V_DIM), lambda h, i: (i, h)),
        out_shape=jax.ShapeDtypeStruct((s, N_HEADS * V_DIM), BF16),
        scratch_shapes=[pltpu.VMEM((t, 2 * t), F32), pltpu.VMEM((t, 2 * t), F32),
                        pltpu.VMEM((8, 2 * t), F32), pltpu.VMEM((8, 2 * t), F32),
                        pltpu.VMEM((1, 2 * t), F32), pltpu.VMEM((8, 2 * t), F32),
                        pltpu.VMEM((V_DIM, 2 * t), F32)],
        compiler_params=_params(2),
        name="prompt_attn",
    )(*lams, gsub, qt, k, vt)


def _sample_attn_kernel(lq1_ref, lk1_ref, lq2_ref, lk2_ref, gsub_ref, q_ref, kn_ref, vn_ref,
                        ck_ref, cv_ref, o_ref, *, heads, t, lam_init):
    lam = _lambda(lq1_ref, lk1_ref, lq2_ref, lk2_ref, lam_init)
    for hd in range(heads):
        sl = slice(hd * V_DIM, (hd + 1) * V_DIM)
        qs = _stack_subheads(q_ref[:, sl])
        head_rows = pl.ds(hd, ck_ref.shape[1] // heads, stride=heads)
        s_c = _scores(qs, ck_ref[0, head_rows, :].astype(BF16))
        s_n = _scores(qs, kn_ref[:, sl])
        m = jnp.maximum(jnp.max(s_c, axis=1, keepdims=True), jnp.max(s_n, axis=1, keepdims=True))
        p_c = jnp.exp2(s_c - m)
        p_n = jnp.exp2(s_n - m)
        l = jnp.sum(p_c, axis=1, keepdims=True) + jnp.sum(p_n, axis=1, keepdims=True)
        acc = jnp.dot(p_c.astype(BF16), cv_ref[0, head_rows, :].astype(BF16), preferred_element_type=F32)
        acc = acc + jnp.dot(p_n.astype(BF16), vn_ref[:, sl], preferred_element_type=F32)
        o_ref[:, sl] = _finish_head(acc / l, t, lam, gsub_ref[...], lam_init).astype(o_ref.dtype)


def _sample_attention(lams, gsub, q, k_new, v_new, cache_k, cache_v, *, lam_init):
    nb, rows, _ = cache_k.shape
    heads = N_HEADS
    t = q.shape[0] // nb
    vec_spec = pl.BlockSpec((1, HEAD_DIM), lambda b: (0, 0))
    new_spec = pl.BlockSpec((t, heads * V_DIM), lambda b: (b, 0))
    cache_spec = pl.BlockSpec((1, rows, V_DIM), lambda b: (b, 0, 0))
    return pl.pallas_call(
        functools.partial(_sample_attn_kernel, heads=heads, t=t, lam_init=lam_init),
        grid=(nb,),
        in_specs=[vec_spec] * 4 + [pl.BlockSpec((1, V_DIM), lambda b: (0, 0)),
                                   new_spec, new_spec, new_spec, cache_spec, cache_spec],
        out_specs=new_spec,
        out_shape=jax.ShapeDtypeStruct(q.shape, BF16),
        compiler_params=_params(1),
        name="sample_attn",
    )(*lams, gsub, q, k_new, v_new, cache_k, cache_v)


def _merge_kernel(h_ref, attn_ref, u_ref, vn_ref, ga_ref, gb_ref, ws_ref, bs_ref,
                  wpa_ref, wpb_ref, wo_ref, gpost_ref, o_ref, sgu_scr, *, tm, chunk):
    r = lax.broadcasted_iota(jnp.int32, (chunk, chunk), 0)
    c = lax.broadcasted_iota(jnp.int32, (chunk, chunk), 1)
    for g in range(N_GROUPS):
        w = jnp.where(c <= r, ws_ref[g], 0.0).astype(BF16)
        bias = bs_ref[g]
        gl = slice(g * GROUP_DIM, (g + 1) * GROUP_DIM)
        for ci in range(tm // chunk):
            rs = slice(ci * chunk, (ci + 1) * chunk)
            mixed = jnp.dot(w, vn_ref[rs, gl].astype(BF16), preferred_element_type=F32) + bias
            sgu_scr[rs, gl] = (u_ref[rs, gl].astype(F32) * mixed).astype(BF16)
    pa = jnp.dot(attn_ref[...], wpa_ref[...], preferred_element_type=F32)
    pb = jnp.dot(sgu_scr[...], wpb_ref[...], preferred_element_type=F32)
    merged = ga_ref[...].astype(F32) * pa + gb_ref[...].astype(F32) * pb
    mix = jnp.dot(merged.astype(BF16), wo_ref[...], preferred_element_type=F32)
    o_ref[...] = h_ref[...] + _rms(mix, gpost_ref[...])


def _merge(h, attn, u, vn, ga, gb, ws, bs, wpa, wpb, wo, gpost, *, tm, chunk):
    rows = h.shape[0]
    row_spec = pl.BlockSpec((tm, D_MODEL), lambda i: (i, 0))
    return pl.pallas_call(
        functools.partial(_merge_kernel, tm=tm, chunk=chunk),
        grid=(rows // tm,),
        in_specs=[row_spec] * 6 + [_resident(a.shape) for a in (ws, bs, wpa, wpb, wo, gpost)],
        out_specs=row_spec,
        out_shape=jax.ShapeDtypeStruct((rows, D_MODEL), F32),
        scratch_shapes=[pltpu.VMEM((tm, D_B), BF16)],
        compiler_params=_params(1),
        name="merge",
    )(h, attn, u, vn, ga, gb, ws, bs, wpa, wpb, wo, gpost)


def _row_tile(rows, target):
    tm = min(rows, target)
    assert rows % tm == 0, (rows, tm)
    return tm


def kernel(x_prompt, x_sample, cache_k, cache_v, ln_ffn1_pre, w_ffn1_in, w_ffn1_out, ln_ffn1_post, ln_mix_pre, w_in, lambda_q1, lambda_k1, lambda_q2, lambda_k2, ln_subln, ln_sgu, w_spatial, b_spatial, w_proj_a, w_proj_b, w_out, ln_mix_post, ln_ffn2_pre, w_ffn2_in, w_ffn2_out, ln_ffn2_post):
    depth = w_in.shape[0]
    assert depth == 1, "single-layer trunk"
    batch, seq, _ = x_prompt.shape
    dec_batch, dec_seq, _ = x_sample.shape
    past = cache_k.shape[2]
    assert batch == 1 and seq % GMLP_CHUNK == 0 and dec_seq <= GMLP_CHUNK
    lam_init = 0.8 - 0.6 * math.exp(-0.3 * 0)

    row = lambda g: g[0][None, :]
    ffn1 = (row(ln_ffn1_pre), w_ffn1_in[0, :, :D_FF].astype(BF16), w_ffn1_in[0, :, D_FF:].astype(BF16),
            w_ffn1_out[0].astype(BF16), row(ln_ffn1_post))
    ffn2 = (row(ln_ffn2_pre), w_ffn2_in[0, :, :D_FF].astype(BF16), w_ffn2_in[0, :, D_FF:].astype(BF16),
            w_ffn2_out[0].astype(BF16), row(ln_ffn2_post))
    w_in_b = w_in[0].astype(BF16)
    wpa, wpb, wo = w_proj_a[0].astype(BF16), w_proj_b[0].astype(BF16), w_out[0].astype(BF16)
    lams = (lambda_q1, lambda_k1, lambda_q2, lambda_k2)
    gsub = row(ln_subln)

    def layer(x, pos, chunk, attend, tm, vn_dtype, transposed_qv):
        h = _ffn(x, *ffn1, tm=tm)
        q, kf, kb, vf, vb, u, vn, ga, gb = _mix_in(
            h, row(ln_mix_pre), w_in_b, row(ln_sgu), _rope_tables(pos), tm=tm, vn_dtype=vn_dtype,
            transposed_qv=transposed_qv)
        attn = attend(q, kb, vb)
        ws = w_spatial[0, :, :chunk, :chunk]
        bs = b_spatial[0, :, :chunk, None]
        h = _merge(h, attn, u, vn, ga, gb, ws, bs, wpa, wpb, wo, row(ln_mix_post), tm=tm, chunk=chunk)
        y = _ffn(h, *ffn2, tm=tm)
        return y, kf, vf, vn

    yp, kp, vp, _ = layer(
        x_prompt.reshape(seq, D_MODEL), jnp.arange(seq), GMLP_CHUNK,
        lambda qt, k, vt: _prompt_attention(lams, gsub, qt, k, vt, lam_init=lam_init),
        _row_tile(seq, 512), BF16, True)

    rows_s = dec_batch * dec_seq
    ys, ks, vs, gs = layer(
        x_sample.reshape(rows_s, D_MODEL), jnp.tile(past + jnp.arange(dec_seq), dec_batch), dec_seq,
        lambda q, k, v: _sample_attention(
            lams, gsub, q, k, v, cache_k[0].reshape(dec_batch, past * N_HEADS, V_DIM),
            cache_v[0].reshape(dec_batch, past * N_HEADS, V_DIM), lam_init=lam_init),
        _row_tile(rows_s, 256), F32, False)

    return (yp.reshape(batch, seq, D_MODEL),
            ys.reshape(dec_batch, dec_seq, D_MODEL),
            kp.reshape(1, batch, seq, N_HEADS, V_DIM),
            vp.reshape(1, batch, seq, N_HEADS, V_DIM),
            ks.reshape(1, dec_batch, dec_seq, N_HEADS, V_DIM),
            vs.reshape(1, dec_batch, dec_seq, N_HEADS, V_DIM),
            gs.reshape(1, dec_batch, dec_seq, D_B))
```

```python
import functools
import math

import jax
import jax.numpy as jnp
from jax import lax
from jax.experimental import pallas as pl
from jax.experimental.pallas import tpu as pltpu

F32 = jnp.float32
BF16 = jnp.bfloat16

D_MODEL = 1024
N_HEADS = 8
HEAD_DIM = 64
V_DIM = 2 * HEAD_DIM
ROPE_DIM = HEAD_DIM // 4
ROPE_THETA = 500000.0
CHUNK = 64
GMLP_CHUNK = 128
N_GROUPS = 4
D_B = 1024
GROUP_DIM = D_B // N_GROUPS
D_FF = 2816
EPS = 1e-6
SCALE = HEAD_DIM ** -0.5
LOG2E = 1.4426950408889634
NEG_INF = -1e30
N_SEG = 7

LANES = 128
VMEM_LIMIT_BYTES = 60 * 1024 * 1024


def _rms(x, g):
    return x * lax.rsqrt(jnp.mean(x * x, axis=-1, keepdims=True) + EPS) * g


def _sigmoid(x):
    return 1.0 / (1.0 + jnp.exp(-x))


def _resident(shape):
    zeros = (0,) * len(shape)
    return pl.BlockSpec(shape, lambda *_: zeros, pipeline_mode=pl.Buffered(1))


def _params(n_axes, flags=None):
    return pltpu.CompilerParams(
        dimension_semantics=("arbitrary",) * n_axes, vmem_limit_bytes=VMEM_LIMIT_BYTES, flags=flags)


def _ffn_kernel(x_ref, gpre_ref, wa_ref, wb_ref, wo_ref, gpost_ref, o_ref):
    x = x_ref[...]
    n = _rms(x, gpre_ref[...]).astype(BF16)
    a = jnp.dot(n, wa_ref[...], preferred_element_type=F32)
    b = jnp.dot(n, wb_ref[...], preferred_element_type=F32)
    g = (a * _sigmoid(a) * b).astype(BF16)
    f = jnp.dot(g, wo_ref[...], preferred_element_type=F32)
    o_ref[...] = x + 0.5 * _rms(f, gpost_ref[...])


def _ffn(x, gpre, wa, wb, wo, gpost, *, tm):
    rows = x.shape[0]
    row_spec = pl.BlockSpec((tm, D_MODEL), lambda i: (i, 0))
    return pl.pallas_call(
        _ffn_kernel,
        grid=(rows // tm,),
        in_specs=[row_spec, _resident(gpre.shape), _resident(wa.shape), _resident(wb.shape),
                  _resident(wo.shape), _resident(gpost.shape)],
        out_specs=row_spec,
        out_shape=jax.ShapeDtypeStruct((rows, D_MODEL), F32),
        compiler_params=_params(1),
        name="ffn",
    )(x, gpre, wa, wb, wo, gpost)


def _gelu(x):
    return jax.nn.gelu(x, approximate=True)


def _mix_in_kernel(h_ref, gpre_ref, w_ref, gsgu_ref, rc_ref, rs1_ref, rs2_ref,
                   q_ref, kf_ref, kb_ref, vf_ref, vb_ref, u_ref, vn_ref, ga_ref, gb_ref, *, transposed_qv):
    n = _rms(h_ref[...], gpre_ref[...]).astype(BF16)

    def put_qv(ref, hd, x):
        if transposed_qv:
            ref[0, hd * V_DIM:(hd + 1) * V_DIM, :] = x.T.astype(BF16)
        else:
            ref[:, hd * V_DIM:(hd + 1) * V_DIM] = x.astype(BF16)

    def seg(j):
        return jnp.dot(n, w_ref[:, j * D_MODEL:(j + 1) * D_MODEL], preferred_element_type=F32)

    rc, rs1, rs2 = rc_ref[...], rs1_ref[...], rs2_ref[...]

    def rope(x):
        return x * rc + pltpu.roll(x, ROPE_DIM // 2, 1) * rs1 + pltpu.roll(x, LANES - ROPE_DIM // 2, 1) * rs2

    q = seg(0)
    for hd in range(N_HEADS):
        sl = slice(hd * V_DIM, (hd + 1) * V_DIM)
        put_qv(q_ref, hd, rope(q[:, sl]) * (SCALE * LOG2E))
    k = seg(1)
    for hd in range(N_HEADS):
        sl = slice(hd * V_DIM, (hd + 1) * V_DIM)
        kr = rope(k[:, sl])
        kf_ref[:, sl] = kr
        kb_ref[:, sl] = kr.astype(BF16)
    v = seg(2)
    vf_ref[...] = v
    for hd in range(N_HEADS):
        put_qv(vb_ref, hd, v[:, hd * V_DIM:(hd + 1) * V_DIM])
    u_ref[...] = _gelu(seg(3)).astype(u_ref.dtype)
    vn_ref[...] = _rms(_gelu(seg(4)), gsgu_ref[...]).astype(vn_ref.dtype)
    ga_ref[...] = _sigmoid(seg(5)).astype(ga_ref.dtype)
    gb_ref[...] = _sigmoid(seg(6)).astype(gb_ref.dtype)


def _mix_in(h, gpre, w_in, gsgu, rope_tabs, *, tm, vn_dtype, transposed_qv):
    rows = h.shape[0]
    row_spec = pl.BlockSpec((tm, D_MODEL), lambda i: (i, 0))
    tab_spec = pl.BlockSpec((tm, LANES), lambda i: (i, 0))
    row_out = lambda dt: (row_spec, jax.ShapeDtypeStruct((rows, D_MODEL), dt))
    if transposed_qv:
        qv_out = (pl.BlockSpec((1, D_MODEL, tm), lambda i: (i, 0, 0)),
                  jax.ShapeDtypeStruct((rows // tm, D_MODEL, tm), BF16))
    else:
        qv_out = row_out(BF16)
    outs = [qv_out, row_out(F32), row_out(BF16), row_out(F32), qv_out, row_out(BF16), row_out(vn_dtype),
            row_out(BF16), row_out(BF16)]
    return pl.pallas_call(
        functools.partial(_mix_in_kernel, transposed_qv=transposed_qv),
        grid=(rows // tm,),
        in_specs=[row_spec, _resident(gpre.shape), _resident(w_in.shape), _resident(gsgu.shape),
                  tab_spec, tab_spec, tab_spec],
        out_specs=[o[0] for o in outs],
        out_shape=[o[1] for o in outs],
        compiler_params=_params(1),
        name="mix_in",
    )(h, gpre, w_in, gsgu, *rope_tabs)


def _rope_tables(pos):
    half = ROPE_DIM // 2
    d = jnp.arange(LANES) % HEAD_DIM
    rotary = d < ROPE_DIM
    inv_freq = ROPE_THETA ** (-(2 * (d % half)).astype(F32) / ROPE_DIM)
    ang = pos.astype(F32)[:, None] * jnp.where(rotary, inv_freq, 0.0)[None, :]
    cos, sin = jnp.cos(ang), jnp.sin(ang)
    first = (d < half)[None, :]
    return cos, jnp.where(first, 0.0, sin), jnp.where(first, -sin, 0.0)


def _lambda(lq1_ref, lk1_ref, lq2_ref, lk2_ref, lam_init):
    e1 = jnp.exp(jnp.sum(lq1_ref[...] * lk1_ref[...], axis=1, keepdims=True))
    e2 = jnp.exp(jnp.sum(lq2_ref[...] * lk2_ref[...], axis=1, keepdims=True))
    return e1 - e2 + lam_init


def _stack_subheads(q):
    lane = lax.broadcasted_iota(jnp.int32, q.shape, 1)
    zero = jnp.zeros_like(q)
    return jnp.concatenate([jnp.where(lane < HEAD_DIM, q, zero), jnp.where(lane >= HEAD_DIM, q, zero)], axis=0)


def _scores(qs, kb):
    return lax.dot_general(qs, kb, (((1,), (1,)), ((), ())), preferred_element_type=F32)


def _finish_head(o, t, lam, gsub, lam_init):
    d = o[:t] - lam * o[t:]
    return _rms(d, gsub) * (1.0 - lam_init)


def _attn_kernel(lq1_ref, lk1_ref, lq2_ref, lk2_ref, gsub_ref, q_ref, k_ref, v_ref, o_ref,
                 sa_scr, sb_scr, mxa_scr, mxb_scr, m_scr, l_scr, acc_scr, *, t, lam_init):
    i = pl.program_id(1)
    sub = 8
    g = t // sub
    qt = q_ref[0]
    feat = lax.broadcasted_iota(jnp.int32, qt.shape, 0)
    zero = jnp.zeros_like(qt)
    qs = jnp.concatenate([jnp.where(feat < HEAD_DIM, qt, zero), jnp.where(feat >= HEAD_DIM, qt, zero)], axis=1)
    m_scr[...] = jnp.full(m_scr.shape, NEG_INF, F32)
    l_scr[...] = jnp.zeros(l_scr.shape, F32)
    acc_scr[...] = jnp.zeros(acc_scr.shape, F32)

    def qk_stage(j, s_ref, mx_ref):
        s = jnp.dot(k_ref[pl.ds(pl.multiple_of(j * t, t), t), :], qs, preferred_element_type=F32)
        s_ref[...] = s
        mx_ref[...] = jnp.max(s.reshape(g, sub, 2 * t), axis=0)

    def softmax_stage(j, s_ref, mx_ref, masked):
        vt = v_ref[j]
        for half in range(2):
            cs = slice(half * t, (half + 1) * t)
            s3 = s_ref[:, cs].reshape(g, sub, t)
            if masked:
                shape = (g, sub, t)
                key_chunk = lax.shift_right_logical(
                    lax.broadcasted_iota(jnp.int32, shape, 0), (CHUNK // sub).bit_length() - 1)
                col_chunk = lax.shift_right_logical(
                    lax.broadcasted_iota(jnp.int32, shape, 2), CHUNK.bit_length() - 1)
                s3 = jnp.where(key_chunk <= col_chunk, s3, NEG_INF)
                m8 = jnp.max(s3, axis=0)
            else:
                m8 = mx_ref[:, cs]
            m_prev = m_scr[:, cs]
            m_new = jnp.maximum(m_prev, jnp.max(m8, axis=0, keepdims=True))
            alpha = jnp.exp2(m_prev - m_new)
            p3 = jnp.exp2(s3 - jnp.broadcast_to(m_new, (sub, t))[None])
            l_scr[:, cs] = alpha * l_scr[:, cs] + jnp.sum(p3, axis=0)
            pv = jnp.dot(vt, p3.reshape(t, t).astype(BF16), preferred_element_type=F32)
            acc_scr[:, cs] = alpha * acc_scr[:, cs] + pv
            m_scr[:, cs] = m_new

    qk_stage(0, sa_scr, mxa_scr)

    def pair(j):
        qk_stage(j + 1, sb_scr, mxb_scr)
        softmax_stage(j, sa_scr, mxa_scr, False)
        qk_stage(j + 2, sa_scr, mxa_scr)
        softmax_stage(j + 1, sb_scr, mxb_scr, False)

    def body(jj, carry):
        pair(4 * jj)
        pair(4 * jj + 2)
        return carry

    lax.fori_loop(0, i // 4, body, 0)

    @pl.when(i % 4 >= 2)
    def _():
        pair(4 * (i // 4))

    @pl.when(i % 2 == 1)
    def _():
        qk_stage(i, sb_scr, mxb_scr)
        softmax_stage(i - 1, sa_scr, mxa_scr, False)
        softmax_stage(i, sb_scr, mxb_scr, True)

    @pl.when(i % 2 == 0)
    def _():
        softmax_stage(i, sa_scr, mxa_scr, True)

    lam = _lambda(lq1_ref, lk1_ref, lq2_ref, lk2_ref, lam_init)
    o = acc_scr[...] / jnp.sum(l_scr[...], axis=0, keepdims=True)
    d = (o[:, :t] - lam * o[:, t:]).T
    o_ref[...] = (_rms(d, gsub_ref[...]) * (1.0 - lam_init)).astype(o_ref.dtype)


def _prompt_attention(lams, gsub, qt, k, vt, *, lam_init):
    nt, _, t = qt.shape
    s = nt * t
    vec_spec = pl.BlockSpec((1, HEAD_DIM), lambda h, i: (0, 0))
    return pl.pallas_call(
        functools.partial(_attn_kernel, t=t, lam_init=lam_init),
        grid=(N_HEADS, nt),
        in_specs=[vec_spec] * 4 + [
            pl.BlockSpec((1, V_DIM), lambda h, i: (0, 0)),
            pl.BlockSpec((1, V_DIM, t), lambda h, i: (i, h, 0)),
            pl.BlockSpec((s, V_DIM), lambda h, i: (0, h)),
            pl.BlockSpec((nt, V_DIM, t), lambda h, i: (0, h, 0)),
        ],
        out_specs=pl.BlockSpec((t, V_DIM), lambda h, i: (i, h)),
        out_shape=jax.ShapeDtypeStruct((s, N_HEADS * V_DIM), BF16),
        scratch_shapes=[pltpu.VMEM((t, 2 * t), F32), pltpu.VMEM((t, 2 * t), F32),
                        pltpu.VMEM((8, 2 * t), F32), pltpu.VMEM((8, 2 * t), F32),
                        pltpu.VMEM((1, 2 * t), F32), pltpu.VMEM((8, 2 * t), F32),
                        pltpu.VMEM((V_DIM, 2 * t), F32)],
        compiler_params=_params(2),
        name="prompt_attn",
    )(*lams, gsub, qt, k, vt)


def _sample_attn_kernel(lq1_ref, lk1_ref, lq2_ref, lk2_ref, gsub_ref, q_ref, kn_ref, vn_ref,
                        ck_ref, cv_ref, o_ref, *, heads, t, lam_init):
    lam = _lambda(lq1_ref, lk1_ref, lq2_ref, lk2_ref, lam_init)
    for hd in range(heads):
        sl = slice(hd * V_DIM, (hd + 1) * V_DIM)
        qs = _stack_subheads(q_ref[:, sl])
        head_rows = pl.ds(hd, ck_ref.shape[1] // heads, stride=heads)
        s_c = _scores(qs, ck_ref[0, head_rows, :].astype(BF16))
        s_n = _scores(qs, kn_ref[:, sl])
        m = jnp.maximum(jnp.max(s_c, axis=1, keepdims=True), jnp.max(s_n, axis=1, keepdims=True))
        p_c = jnp.exp2(s_c - m)
        p_n = jnp.exp2(s_n - m)
        l = jnp.sum(p_c, axis=1, keepdims=True) + jnp.sum(p_n, axis=1, keepdims=True)
        acc = jnp.dot(p_c.astype(BF16), cv_ref[0, head_rows, :].astype(BF16), preferred_element_type=F32)
        acc = acc + jnp.dot(p_n.astype(BF16), vn_ref[:, sl], preferred_element_type=F32)
        o_ref[:, sl] = _finish_head(acc / l, t, lam, gsub_ref[...], lam_init).astype(o_ref.dtype)


def _sample_attention(lams, gsub, q, k_new, v_new, cache_k, cache_v, *, lam_init):
    nb, rows, _ = cache_k.shape
    heads = N_HEADS
    t = q.shape[0] // nb
    vec_spec = pl.BlockSpec((1, HEAD_DIM), lambda b: (0, 0))
    new_spec = pl.BlockSpec((t, heads * V_DIM), lambda b: (b, 0))
    cache_spec = pl.BlockSpec((1, rows, V_DIM), lambda b: (b, 0, 0))
    return pl.pallas_call(
        functools.partial(_sample_attn_kernel, heads=heads, t=t, lam_init=lam_init),
        grid=(nb,),
        in_specs=[vec_spec] * 4 + [pl.BlockSpec((1, V_DIM), lambda b: (0, 0)),
                                   new_spec, new_spec, new_spec, cache_spec, cache_spec],
        out_specs=new_spec,
        out_shape=jax.ShapeDtypeStruct(q.shape, BF16),
        compiler_params=_params(1),
        name="sample_attn",
    )(*lams, gsub, q, k_new, v_new, cache_k, cache_v)


def _merge_kernel(h_ref, attn_ref, u_ref, vn_ref, ga_ref, gb_ref, ws_ref, bs_ref,
                  wpa_ref, wpb_ref, wo_ref, gpost_ref, o_ref, sgu_scr, *, tm, chunk):
    r = lax.broadcasted_iota(jnp.int32, (chunk, chunk), 0)
    c = lax.broadcasted_iota(jnp.int32, (chunk, chunk), 1)
    for g in range(N_GROUPS):
        w = jnp.where(c <= r, ws_ref[g], 0.0).astype(BF16)
        bias = bs_ref[g]
        gl = slice(g * GROUP_DIM, (g + 1) * GROUP_DIM)
        for ci in range(tm // chunk):
            rs = slice(ci * chunk, (ci + 1) * chunk)
            mixed = jnp.dot(w, vn_ref[rs, gl].astype(BF16), preferred_element_type=F32) + bias
            sgu_scr[rs, gl] = (u_ref[rs, gl].astype(F32) * mixed).astype(BF16)
    pa = jnp.dot(attn_ref[...], wpa_ref[...], preferred_element_type=F32)
    pb = jnp.dot(sgu_scr[...], wpb_ref[...], preferred_element_type=F32)
    merged = ga_ref[...].astype(F32) * pa + gb_ref[...].astype(F32) * pb
    mix = jnp.dot(merged.astype(BF16), wo_ref[...], preferred_element_type=F32)
    o_ref[...] = h_ref[...] + _rms(mix, gpost_ref[...])


def _merge(h, attn, u, vn, ga, gb, ws, bs, wpa, wpb, wo, gpost, *, tm, chunk):
    rows = h.shape[0]
    row_spec = pl.BlockSpec((tm, D_MODEL), lambda i: (i, 0))
    return pl.pallas_call(
        functools.partial(_merge_kernel, tm=tm, chunk=chunk),
        grid=(rows // tm,),
        in_specs=[row_spec] * 6 + [_resident(a.shape) for a in (ws, bs, wpa, wpb, wo, gpost)],
        out_specs=row_spec,
        out_shape=jax.ShapeDtypeStruct((rows, D_MODEL), F32),
        scratch_shapes=[pltpu.VMEM((tm, D_B), BF16)],
        compiler_params=_params(1),
        name="merge",
    )(h, attn, u, vn, ga, gb, ws, bs, wpa, wpb, wo, gpost)


def _row_tile(rows, target):
    tm = min(rows, target)
    assert rows % tm == 0, (rows, tm)
    return tm


def kernel(x_prompt, x_sample, cache_k, cache_v, ln_ffn1_pre, w_ffn1_in, w_ffn1_out, ln_ffn1_post, ln_mix_pre, w_in, lambda_q1, lambda_k1, lambda_q2, lambda_k2, ln_subln, ln_sgu, w_spatial, b_spatial, w_proj_a, w_proj_b, w_out, ln_mix_post, ln_ffn2_pre, w_ffn2_in, w_ffn2_out, ln_ffn2_post):
    depth = w_in.shape[0]
    assert depth == 1, "single-layer trunk"
    batch, seq, _ = x_prompt.shape
    dec_batch, dec_seq, _ = x_sample.shape
    past = cache_k.shape[2]
    assert batch == 1 and seq % GMLP_CHUNK == 0 and dec_seq <= GMLP_CHUNK
    lam_init = 0.8 - 0.6 * math.exp(-0.3 * 0)

    row = lambda g: g[0][None, :]
    ffn1 = (row(ln_ffn1_pre), w_ffn1_in[0, :, :D_FF].astype(BF16), w_ffn1_in[0, :, D_FF:].astype(BF16),
            w_ffn1_out[0].astype(BF16), row(ln_ffn1_post))
    ffn2 = (row(ln_ffn2_pre), w_ffn2_in[0, :, :D_FF].astype(BF16), w_ffn2_in[0, :, D_FF:].astype(BF16),
            w_ffn2_out[0].astype(BF16), row(ln_ffn2_post))
    w_in_b = w_in[0].astype(BF16)
    wpa, wpb, wo = w_proj_a[0].astype(BF16), w_proj_b[0].astype(BF16), w_out[0].astype(BF16)
    lams = (lambda_q1, lambda_k1, lambda_q2, lambda_k2)
    gsub = row(ln_subln)

    def layer(x, pos, chunk, attend, tm, vn_dtype, transposed_qv):
        h = _ffn(x, *ffn1, tm=tm)
        q, kf, kb, vf, vb, u, vn, ga, gb = _mix_in(
            h, row(ln_mix_pre), w_in_b, row(ln_sgu), _rope_tables(pos), tm=tm, vn_dtype=vn_dtype,
            transposed_qv=transposed_qv)
        attn = attend(q, kb, vb)
        ws = w_spatial[0, :, :chunk, :chunk]
        bs = b_spatial[0, :, :chunk, None]
        h = _merge(h, attn, u, vn, ga, gb, ws, bs, wpa, wpb, wo, row(ln_mix_post), tm=tm, chunk=chunk)
        y = _ffn(h, *ffn2, tm=tm)
        return y, kf, vf, vn

    yp, kp, vp, _ = layer(
        x_prompt.reshape(seq, D_MODEL), jnp.arange(seq), GMLP_CHUNK,
        lambda qt, k, vt: _prompt_attention(lams, gsub, qt, k, vt, lam_init=lam_init),
        _row_tile(seq, 512), BF16, True)

    rows_s = dec_batch * dec_seq
    ys, ks, vs, gs = layer(
        x_sample.reshape(rows_s, D_MODEL), jnp.tile(past + jnp.arange(dec_seq), dec_batch), dec_seq,
        lambda q, k, v: _sample_attention(
            lams, gsub, q, k, v, cache_k[0].reshape(dec_batch, past * N_HEADS, V_DIM),
            cache_v[0].reshape(dec_batch, past * N_HEADS, V_DIM), lam_init=lam_init),
        _row_tile(rows_s, 256), F32, False)

    return (yp.reshape(batch, seq, D_MODEL),
            ys.reshape(dec_batch, dec_seq, D_MODEL),
            kp.reshape(1, batch, seq, N_HEADS, V_DIM),
            vp.reshape(1, batch, seq, N_HEADS, V_DIM),
            ks.reshape(1, dec_batch, dec_seq, N_HEADS, V_DIM),
            vs.reshape(1, dec_batch, dec_seq, N_HEADS, V_DIM),
            gs.reshape(1, dec_batch, dec_seq, D_B))
```

```python
import functools
import math

import jax
import jax.numpy as jnp
from jax import lax
from jax.experimental import pallas as pl
from jax.experimental.pallas import tpu as pltpu

F32 = jnp.float32
BF16 = jnp.bfloat16

D_MODEL = 1024
N_HEADS = 8
HEAD_DIM = 64
V_DIM = 2 * HEAD_DIM
ROPE_DIM = HEAD_DIM // 4
ROPE_THETA = 500000.0
CHUNK = 64
GMLP_CHUNK = 128
N_GROUPS = 4
D_B = 1024
GROUP_DIM = D_B // N_GROUPS
D_FF = 2816
EPS = 1e-6
SCALE = HEAD_DIM ** -0.5
LOG2E = 1.4426950408889634
NEG_INF = -1e30
N_SEG = 7

LANES = 128
VMEM_LIMIT_BYTES = 60 * 1024 * 1024


def _rms(x, g):
    return x * lax.rsqrt(jnp.mean(x * x, axis=-1, keepdims=True) + EPS) * g


def _sigmoid(x):
    return 1.0 / (1.0 + jnp.exp(-x))


def _resident(shape):
    zeros = (0,) * len(shape)
    return pl.BlockSpec(shape, lambda *_: zeros, pipeline_mode=pl.Buffered(1))


def _params(n_axes, flags=None):
    return pltpu.CompilerParams(
        dimension_semantics=("arbitrary",) * n_axes, vmem_limit_bytes=VMEM_LIMIT_BYTES, flags=flags)


def _ffn_kernel(x_ref, gpre_ref, wa_ref, wb_ref, wo_ref, gpost_ref, o_ref):
    x = x_ref[...]
    n = _rms(x, gpre_ref[...]).astype(BF16)
    a = jnp.dot(n, wa_ref[...], preferred_element_type=F32)
    b = jnp.dot(n, wb_ref[...], preferred_element_type=F32)
    g = (a * _sigmoid(a) * b).astype(BF16)
    f = jnp.dot(g, wo_ref[...], preferred_element_type=F32)
    o_ref[...] = x + 0.5 * _rms(f, gpost_ref[...])


def _ffn(x, gpre, wa, wb, wo, gpost, *, tm):
    rows = x.shape[0]
    row_spec = pl.BlockSpec((tm, D_MODEL), lambda i: (i, 0))
    return pl.pallas_call(
        _ffn_kernel,
        grid=(rows // tm,),
        in_specs=[row_spec, _resident(gpre.shape), _resident(wa.shape), _resident(wb.shape),
                  _resident(wo.shape), _resident(gpost.shape)],
        out_specs=row_spec,
        out_shape=jax.ShapeDtypeStruct((rows, D_MODEL), F32),
        compiler_params=_params(1),
        name="ffn",
    )(x, gpre, wa, wb, wo, gpost)


def _gelu(x):
    return jax.nn.gelu(x, approximate=True)


def _mix_in_kernel(h_ref, gpre_ref, w_ref, gsgu_ref, rc_ref, rs1_ref, rs2_ref,
                   q_ref, kf_ref, kb_ref, vf_ref, vb_ref, u_ref, vn_ref, ga_ref, gb_ref, *, transposed_qv):
    n = _rms(h_ref[...], gpre_ref[...]).astype(BF16)

    def put_qv(ref, hd, x):
        if transposed_qv:
            ref[0, hd * V_DIM:(hd + 1) * V_DIM, :] = x.T.astype(BF16)
        else:
            ref[:, hd * V_DIM:(hd + 1) * V_DIM] = x.astype(BF16)

    def seg(j):
        return jnp.dot(n, w_ref[:, j * D_MODEL:(j + 1) * D_MODEL], preferred_element_type=F32)

    rc, rs1, rs2 = rc_ref[...], rs1_ref[...], rs2_ref[...]

    def rope(x):
        return x * rc + pltpu.roll(x, ROPE_DIM // 2, 1) * rs1 + pltpu.roll(x, LANES - ROPE_DIM // 2, 1) * rs2

    q = seg(0)
    for hd in range(N_HEADS):
        sl = slice(hd * V_DIM, (hd + 1) * V_DIM)
        put_qv(q_ref, hd, rope(q[:, sl]) * (SCALE * LOG2E))
    k = seg(1)
    for hd in range(N_HEADS):
        sl = slice(hd * V_DIM, (hd + 1) * V_DIM)
        kr = rope(k[:, sl])
        kf_ref[:, sl] = kr
        kb_ref[:, sl] = kr.astype(BF16)
    v = seg(2)
    vf_ref[...] = v
    for hd in range(N_HEADS):
        put_qv(vb_ref, hd, v[:, hd * V_DIM:(hd + 1) * V_DIM])
    u_ref[...] = _gelu(seg(3)).astype(u_ref.dtype)
    vn_ref[...] = _rms(_gelu(seg(4)), gsgu_ref[...]).astype(vn_ref.dtype)
    ga_ref[...] = _sigmoid(seg(5)).astype(ga_ref.dtype)
    gb_ref[...] = _sigmoid(seg(6)).astype(gb_ref.dtype)


def _mix_in(h, gpre, w_in, gsgu, rope_tabs, *, tm, vn_dtype, transposed_qv):
    rows = h.shape[0]
    row_spec = pl.BlockSpec((tm, D_MODEL), lambda i: (i, 0))
    tab_spec = pl.BlockSpec((tm, LANES), lambda i: (i, 0))
    row_out = lambda dt: (row_spec, jax.ShapeDtypeStruct((rows, D_MODEL), dt))
    if transposed_qv:
        qv_out = (pl.BlockSpec((1, D_MODEL, tm), lambda i: (i, 0, 0)),
                  jax.ShapeDtypeStruct((rows // tm, D_MODEL, tm), BF16))
    else:
        qv_out = row_out(BF16)
    outs = [qv_out, row_out(F32), row_out(BF16), row_out(F32), qv_out, row_out(BF16), row_out(vn_dtype),
            row_out(BF16), row_out(BF16)]
    return pl.pallas_call(
        functools.partial(_mix_in_kernel, transposed_qv=transposed_qv),
        grid=(rows // tm,),
        in_specs=[row_spec, _resident(gpre.shape), _resident(w_in.shape), _resident(gsgu.shape),
                  tab_spec, tab_spec, tab_spec],
        out_specs=[o[0] for o in outs],
        out_shape=[o[1] for o in outs],
        compiler_params=_params(1),
        name="mix_in",
    )(h, gpre, w_in, gsgu, *rope_tabs)


def _rope_tables(pos):
    half = ROPE_DIM // 2
    d = jnp.arange(LANES) % HEAD_DIM
    rotary = d < ROPE_DIM
    inv_freq = ROPE_THETA ** (-(2 * (d % half)).astype(F32) / ROPE_DIM)
    ang = pos.astype(F32)[:, None] * jnp.where(rotary, inv_freq, 0.0)[None, :]
    cos, sin = jnp.cos(ang), jnp.sin(ang)
    first = (d < half)[None, :]
    return cos, jnp.where(first, 0.0, sin), jnp.where(first, -sin, 0.0)


def _lambda(lq1_ref, lk1_ref, lq2_ref, lk2_ref, lam_init):
    e1 = jnp.exp(jnp.sum(lq1_ref[...] * lk1_ref[...], axis=1, keepdims=True))
    e2 = jnp.exp(jnp.sum(lq2_ref[...] * lk2_ref[...], axis=1, keepdims=True))
    return e1 - e2 + lam_init


def _stack_subheads(q):
    lane = lax.broadcasted_iota(jnp.int32, q.shape, 1)
    zero = jnp.zeros_like(q)
    return jnp.concatenate([jnp.where(lane < HEAD_DIM, q, zero), jnp.where(lane >= HEAD_DIM, q, zero)], axis=0)


def _scores(qs, kb):
    return lax.dot_general(qs, kb, (((1,), (1,)), ((), ())), preferred_element_type=F32)


def _finish_head(o, t, lam, gsub, lam_init):
    d = o[:t] - lam * o[t:]
    return _rms(d, gsub) * (1.0 - lam_init)


def _attn_kernel(lq1_ref, lk1_ref, lq2_ref, lk2_ref, gsub_ref, q_ref, k_ref, v_ref, o_ref,
                 sa_scr, sb_scr, mxa_scr, mxb_scr, m_scr, l_scr, acc_scr, *, t, lam_init):
    i = pl.program_id(1)
    sub = 8
    g = t // sub
    qt = q_ref[0]
    feat = lax.broadcasted_iota(jnp.int32, qt.shape, 0)
    zero = jnp.zeros_like(qt)
    qs = jnp.concatenate([jnp.where(feat < HEAD_DIM, qt, zero), jnp.where(feat >= HEAD_DIM, qt, zero)], axis=1)
    m_scr[...] = jnp.full(m_scr.shape, NEG_INF, F32)
    l_scr[...] = jnp.zeros(l_scr.shape, F32)
    acc_scr[...] = jnp.zeros(acc_scr.shape, F32)

    def qk_stage(j, s_ref, mx_ref):
        s = jnp.dot(k_ref[pl.ds(pl.multiple_of(j * t, t), t), :], qs, preferred_element_type=F32)
        s_ref[...] = s
        mx_ref[...] = jnp.max(s.reshape(g, sub, 2 * t), axis=0)

    def softmax_stage(j, s_ref, mx_ref, masked):
        vt = v_ref[j]
        w = t // 2
        for part in range(2 * t // w):
            cs = slice(part * w, (part + 1) * w)
            s3 = s_ref[:, cs].reshape(g, sub, w)
            if masked:
                shape = (g, sub, w)
                key_chunk = lax.shift_right_logical(
                    lax.broadcasted_iota(jnp.int32, shape, 0), (CHUNK // sub).bit_length() - 1)
                query = lax.broadcasted_iota(jnp.int32, shape, 2) + (part * w) % t
                col_chunk = lax.shift_right_logical(query, CHUNK.bit_length() - 1)
                s3 = jnp.where(key_chunk <= col_chunk, s3, NEG_INF)
                m8 = jnp.max(s3, axis=0)
            else:
                m8 = mx_ref[:, cs]
            m_prev = m_scr[:, cs]
            m_new = jnp.maximum(m_prev, jnp.max(m8, axis=0, keepdims=True))
            alpha = jnp.exp2(m_prev - m_new)
            p3 = jnp.exp2(s3 - jnp.broadcast_to(m_new, (sub, w))[None])
            l_scr[:, cs] = alpha * l_scr[:, cs] + jnp.sum(p3, axis=0)
            pv = jnp.dot(vt, p3.reshape(t, w).astype(BF16), preferred_element_type=F32)
            acc_scr[:, cs] = alpha * acc_scr[:, cs] + pv
            m_scr[:, cs] = m_new

    qk_stage(0, sa_scr, mxa_scr)

    def pair(j):
        qk_stage(j + 1, sb_scr, mxb_scr)
        softmax_stage(j, sa_scr, mxa_scr, False)
        qk_stage(j + 2, sa_scr, mxa_scr)
        softmax_stage(j + 1, sb_scr, mxb_scr, False)

    def body(jj, carry):
        pair(4 * jj)
        pair(4 * jj + 2)
        return carry

    lax.fori_loop(0, i // 4, body, 0)

    @pl.when(i % 4 >= 2)
    def _():
        pair(4 * (i // 4))

    @pl.when(i % 2 == 1)
    def _():
        qk_stage(i, sb_scr, mxb_scr)
        softmax_stage(i - 1, sa_scr, mxa_scr, False)
        softmax_stage(i, sb_scr, mxb_scr, True)

    @pl.when(i % 2 == 0)
    def _():
        softmax_stage(i, sa_scr, mxa_scr, True)

    lam = _lambda(lq1_ref, lk1_ref, lq2_ref, lk2_ref, lam_init)
    o = acc_scr[...] / jnp.sum(l_scr[...], axis=0, keepdims=True)
    d = (o[:, :t] - lam * o[:, t:]).T
    o_ref[...] = (_rms(d, gsub_ref[...]) * (1.0 - lam_init)).astype(o_ref.dtype)


def _prompt_attention(lams, gsub, qt, k, vt, *, lam_init):
    nt, _, t = qt.shape
    s = nt * t
    vec_spec = pl.BlockSpec((1, HEAD_DIM), lambda h, i: (0, 0))
    return pl.pallas_call(
        functools.partial(_attn_kernel, t=t, lam_init=lam_init),
        grid=(N_HEADS, nt),
        in_specs=[vec_spec] * 4 + [
            pl.BlockSpec((1, V_DIM), lambda h, i: (0, 0)),
            pl.BlockSpec((1, V_DIM, t), lambda h, i: (i, h, 0)),
            pl.BlockSpec((s, V_DIM), lambda h, i: (0, h)),
            pl.BlockSpec((nt, V_DIM, t), lambda h, i: (0, h, 0)),
        ],
        out_specs=pl.BlockSpec((t, V_DIM), lambda h, i: (i, h)),
        out_shape=jax.ShapeDtypeStruct((s, N_HEADS * V_DIM), BF16),
        scratch_shapes=[pltpu.VMEM((t, 2 * t), F32), pltpu.VMEM((t, 2 * t), F32),
                        pltpu.VMEM((8, 2 * t), F32), pltpu.VMEM((8, 2 * t), F32),
                        pltpu.VMEM((1, 2 * t), F32), pltpu.VMEM((8, 2 * t), F32),
                        pltpu.VMEM((V_DIM, 2 * t), F32)],
        compiler_params=_params(2),
        name="prompt_attn",
    )(*lams, gsub, qt, k, vt)


def _sample_attn_kernel(lq1_ref, lk1_ref, lq2_ref, lk2_ref, gsub_ref, q_ref, kn_ref, vn_ref,
                        ck_ref, cv_ref, o_ref, *, heads, t, lam_init):
    lam = _lambda(lq1_ref, lk1_ref, lq2_ref, lk2_ref, lam_init)
    for hd in range(heads):
        sl = slice(hd * V_DIM, (hd + 1) * V_DIM)
        qs = _stack_subheads(q_ref[:, sl])
        head_rows = pl.ds(hd, ck_ref.shape[1] // heads, stride=heads)
        s_c = _scores(qs, ck_ref[0, head_rows, :].astype(BF16))
        s_n = _scores(qs, kn_ref[:, sl])
        m = jnp.maximum(jnp.max(s_c, axis=1, keepdims=True), jnp.max(s_n, axis=1, keepdims=True))
        p_c = jnp.exp2(s_c - m)
        p_n = jnp.exp2(s_n - m)
        l = jnp.sum(p_c, axis=1, keepdims=True) + jnp.sum(p_n, axis=1, keepdims=True)
        acc = jnp.dot(p_c.astype(BF16), cv_ref[0, head_rows, :].astype(BF16), preferred_element_type=F32)
        acc = acc + jnp.dot(p_n.astype(BF16), vn_ref[:, sl], preferred_element_type=F32)
        o_ref[:, sl] = _finish_head(acc / l, t, lam, gsub_ref[...], lam_init).astype(o_ref.dtype)


def _sample_attention(lams, gsub, q, k_new, v_new, cache_k, cache_v, *, lam_init):
    nb, rows, _ = cache_k.shape
    heads = N_HEADS
    t = q.shape[0] // nb
    vec_spec = pl.BlockSpec((1, HEAD_DIM), lambda b: (0, 0))
    new_spec = pl.BlockSpec((t, heads * V_DIM), lambda b: (b, 0))
    cache_spec = pl.BlockSpec((1, rows, V_DIM), lambda b: (b, 0, 0))
    return pl.pallas_call(
        functools.partial(_sample_attn_kernel, heads=heads, t=t, lam_init=lam_init),
        grid=(nb,),
        in_specs=[vec_spec] * 4 + [pl.BlockSpec((1, V_DIM), lambda b: (0, 0)),
                                   new_spec, new_spec, new_spec, cache_spec, cache_spec],
        out_specs=new_spec,
        out_shape=jax.ShapeDtypeStruct(q.shape, BF16),
        compiler_params=_params(1),
        name="sample_attn",
    )(*lams, gsub, q, k_new, v_new, cache_k, cache_v)


def _merge_kernel(h_ref, attn_ref, u_ref, vn_ref, ga_ref, gb_ref, ws_ref, bs_ref,
                  wpa_ref, wpb_ref, wo_ref, gpost_ref, o_ref, sgu_scr, *, tm, chunk):
    r = lax.broadcasted_iota(jnp.int32, (chunk, chunk), 0)
    c = lax.broadcasted_iota(jnp.int32, (chunk, chunk), 1)
    for g in range(N_GROUPS):
        w = jnp.where(c <= r, ws_ref[g], 0.0).astype(BF16)
        bias = bs_ref[g]
        gl = slice(g * GROUP_DIM, (g + 1) * GROUP_DIM)
        for ci in range(tm // chunk):
            rs = slice(ci * chunk, (ci + 1) * chunk)
            mixed = jnp.dot(w, vn_ref[rs, gl].astype(BF16), preferred_element_type=F32) + bias
            sgu_scr[rs, gl] = (u_ref[rs, gl].astype(F32) * mixed).astype(BF16)
    pa = jnp.dot(attn_ref[...], wpa_ref[...], preferred_element_type=F32)
    pb = jnp.dot(sgu_scr[...], wpb_ref[...], preferred_element_type=F32)
    merged = ga_ref[...].astype(F32) * pa + gb_ref[...].astype(F32) * pb
    mix = jnp.dot(merged.astype(BF16), wo_ref[...], preferred_element_type=F32)
    o_ref[...] = h_ref[...] + _rms(mix, gpost_ref[...])


def _merge(h, attn, u, vn, ga, gb, ws, bs, wpa, wpb, wo, gpost, *, tm, chunk):
    rows = h.shape[0]
    row_spec = pl.BlockSpec((tm, D_MODEL), lambda i: (i, 0))
    return pl.pallas_call(
        functools.partial(_merge_kernel, tm=tm, chunk=chunk),
        grid=(rows // tm,),
        in_specs=[row_spec] * 6 + [_resident(a.shape) for a in (ws, bs, wpa, wpb, wo, gpost)],
        out_specs=row_spec,
        out_shape=jax.ShapeDtypeStruct((rows, D_MODEL), F32),
        scratch_shapes=[pltpu.VMEM((tm, D_B), BF16)],
        compiler_params=_params(1),
        name="merge",
    )(h, attn, u, vn, ga, gb, ws, bs, wpa, wpb, wo, gpost)


def _row_tile(rows, target):
    tm = min(rows, target)
    assert rows % tm == 0, (rows, tm)
    return tm


def kernel(x_prompt, x_sample, cache_k, cache_v, ln_ffn1_pre, w_ffn1_in, w_ffn1_out, ln_ffn1_post, ln_mix_pre, w_in, lambda_q1, lambda_k1, lambda_q2, lambda_k2, ln_subln, ln_sgu, w_spatial, b_spatial, w_proj_a, w_proj_b, w_out, ln_mix_post, ln_ffn2_pre, w_ffn2_in, w_ffn2_out, ln_ffn2_post):
    depth = w_in.shape[0]
    assert depth == 1, "single-layer trunk"
    batch, seq, _ = x_prompt.shape
    dec_batch, dec_seq, _ = x_sample.shape
    past = cache_k.shape[2]
    assert batch == 1 and seq % GMLP_CHUNK == 0 and dec_seq <= GMLP_CHUNK
    lam_init = 0.8 - 0.6 * math.exp(-0.3 * 0)

    row = lambda g: g[0][None, :]
    ffn1 = (row(ln_ffn1_pre), w_ffn1_in[0, :, :D_FF].astype(BF16), w_ffn1_in[0, :, D_FF:].astype(BF16),
            w_ffn1_out[0].astype(BF16), row(ln_ffn1_post))
    ffn2 = (row(ln_ffn2_pre), w_ffn2_in[0, :, :D_FF].astype(BF16), w_ffn2_in[0, :, D_FF:].astype(BF16),
            w_ffn2_out[0].astype(BF16), row(ln_ffn2_post))
    w_in_b = w_in[0].astype(BF16)
    wpa, wpb, wo = w_proj_a[0].astype(BF16), w_proj_b[0].astype(BF16), w_out[0].astype(BF16)
    lams = (lambda_q1, lambda_k1, lambda_q2, lambda_k2)
    gsub = row(ln_subln)

    def layer(x, pos, chunk, attend, tm, vn_dtype, transposed_qv):
        h = _ffn(x, *ffn1, tm=tm)
        q, kf, kb, vf, vb, u, vn, ga, gb = _mix_in(
            h, row(ln_mix_pre), w_in_b, row(ln_sgu), _rope_tables(pos), tm=tm, vn_dtype=vn_dtype,
            transposed_qv=transposed_qv)
        attn = attend(q, kb, vb)
        ws = w_spatial[0, :, :chunk, :chunk]
        bs = b_spatial[0, :, :chunk, None]
        h = _merge(h, attn, u, vn, ga, gb, ws, bs, wpa, wpb, wo, row(ln_mix_post), tm=tm, chunk=chunk)
        y = _ffn(h, *ffn2, tm=tm)
        return y, kf, vf, vn

    yp, kp, vp, _ = layer(
        x_prompt.reshape(seq, D_MODEL), jnp.arange(seq), GMLP_CHUNK,
        lambda qt, k, vt: _prompt_attention(lams, gsub, qt, k, vt, lam_init=lam_init),
        _row_tile(seq, 512), BF16, True)

    rows_s = dec_batch * dec_seq
    ys, ks, vs, gs = layer(
        x_sample.reshape(rows_s, D_MODEL), jnp.tile(past + jnp.arange(dec_seq), dec_batch), dec_seq,
        lambda q, k, v: _sample_attention(
            lams, gsub, q, k, v, cache_k[0].reshape(dec_batch, past * N_HEADS, V_DIM),
            cache_v[0].reshape(dec_batch, past * N_HEADS, V_DIM), lam_init=lam_init),
        _row_tile(rows_s, 256), F32, False)

    return (yp.reshape(batch, seq, D_MODEL),
            ys.reshape(dec_batch, dec_seq, D_MODEL),
            kp.reshape(1, batch, seq, N_HEADS, V_DIM),
            vp.reshape(1, batch, seq, N_HEADS, V_DIM),
            ks.reshape(1, dec_batch, dec_seq, N_HEADS, V_DIM),
            vs.reshape(1, dec_batch, dec_seq, N_HEADS, V_DIM),
            gs.reshape(1, dec_batch, dec_seq, D_B))
```
